```python
import jax, jax.numpy as jnp
from jax import lax
import numpy as np

D_MODEL = 4096
BATCH = 4
SEQ = 2048
DEPTH = 4
DEC_BATCH = 16
DEC_SEQ = 64
PAST_LEN = 1024

CHUNK = 64
EPS = 1e-6
CM_CHUNK = 128
CM_HEADS = 8
CM_HEAD_DIM = 128
D_CM = CM_HEADS * CM_HEAD_DIM
POOL_WINDOWS = (2, 4, 8, 16)
POOL_GROUP_DIM = 256
POOL_GROUPS = len(POOL_WINDOWS)
D_POOL = POOL_GROUPS * POOL_GROUP_DIM
POOL_HIST = max(POOL_WINDOWS) - 1
MLA_HEADS = 16
MLA_Q_RANK = 1024
MLA_KV_RANK = 512
MLA_NOPE_DIM = 128
MLA_ROPE_DIM = 64
MLA_V_DIM = 128
D_MLA = MLA_HEADS * MLA_V_DIM
MLA_SCALE = (MLA_NOPE_DIM + MLA_ROPE_DIM) ** -0.5
ROPE_THETA = 10000.0
Q_BLOCK = 128
D_MIX = D_CM + D_POOL + D_MLA
OFF_U = 0
OFF_V = OFF_U + D_CM
OFF_POOL = OFF_V + D_CM
OFF_Q = OFF_POOL + D_POOL
OFF_CKV = OFF_Q + MLA_Q_RANK
OFF_KR = OFF_CKV + MLA_KV_RANK
D_PROJ = OFF_KR + MLA_ROPE_DIM
D_FF = ((8 * D_MODEL // 3 + 255) // 256) * 256

kernel_name = 'hymba_style_streaming_encoder_step'


def rmsnorm(x, g):
    xf = x.astype(jnp.float32)
    r = lax.rsqrt(jnp.mean(xf * xf, axis=-1, keepdims=True) + EPS)
    return (xf * r).astype(x.dtype) * g


def layernorm(x, g, b):
    xf = x.astype(jnp.float32)
    mu = jnp.mean(xf, axis=-1, keepdims=True)
    var = jnp.mean(jnp.square(xf - mu), axis=-1, keepdims=True)
    return ((xf - mu) * lax.rsqrt(var + EPS)).astype(x.dtype) * g + b


def rope(x, pos):
    half = x.shape[-1] // 2
    inv = ROPE_THETA ** (-jnp.arange(half, dtype=jnp.float32) / half)
    ang = pos.astype(jnp.float32)[:, None] * inv[None, :]
    cos = jnp.cos(ang)[None, :, None, :]
    sin = jnp.sin(ang)[None, :, None, :]
    x1 = x[..., :half].astype(jnp.float32)
    x2 = x[..., half:].astype(jnp.float32)
    return jnp.concatenate([x1 * cos - x2 * sin, x2 * cos + x1 * sin], axis=-1).astype(x.dtype)


def chunk_mlp(u, v, ln_g, ln_b, ws, bs):
    B, L, _ = u.shape
    blk = min(L, CM_CHUNK)
    nc = L // blk
    vn = layernorm(v, ln_g, ln_b)
    idx = jnp.arange(blk)
    mask = (idx[None, :] // CHUNK) <= (idx[:, None] // CHUNK)
    w = jnp.where(mask[None], ws[:, :blk, :blk], 0)
    vh = vn.reshape(B, nc, blk, CM_HEADS, CM_HEAD_DIM)
    sv = jnp.einsum('hij,bcjhd->bcihd', w, vh) + bs[:, :blk].T[None, None, :, :, None]
    return u * sv.reshape(B, L, D_CM), vn


def pool_mix(xp, hist, pos, pool_w, pool_scale):
    B, L, _ = xp.shape
    ext_raw = jnp.concatenate([hist, xp], axis=1)
    ext = ext_raw.astype(jnp.float32)
    cs = jnp.concatenate([jnp.zeros_like(ext[:, :1]), jnp.cumsum(ext, axis=1)], axis=1)
    end = cs[:, POOL_HIST + 1:]
    cur = ext[:, POOL_HIST:]
    outs = []
    for g, w in enumerate(POOL_WINDOWS):
        sl = slice(g * POOL_GROUP_DIM, (g + 1) * POOL_GROUP_DIM)
        start = cs[:, POOL_HIST + 1 - w: POOL_HIST + 1 - w + L, sl]
        cnt = jnp.minimum(w, pos + 1).astype(jnp.float32)[None, :, None]
        outs.append((end[..., sl] - start) / cnt - cur[..., sl])
    m = jnp.stack(outs, axis=2).astype(xp.dtype)
    y = jnp.einsum('blgc,gcd->blgd', m, pool_w).reshape(B, L, D_POOL) * pool_scale
    return y, ext_raw[:, -POOL_HIST:]


def mla_attend(q_nope, q_rope, qpos, k_nope, k_rope, v, kpos):
    s = (jnp.einsum('bqhd,bkhd->bhqk', q_nope, k_nope)
         + jnp.einsum('bqhr,bkr->bhqk', q_rope, k_rope)).astype(jnp.float32) * MLA_SCALE
    mask = (kpos[None, :] // CHUNK) <= (qpos[:, None] // CHUNK)
    s = jnp.where(mask[None, None], s, -jnp.inf)
    p = jax.nn.softmax(s, axis=-1).astype(v.dtype)
    return jnp.einsum('bhqk,bkhd->bqhd', p, v)


def mla(q_lat, ckv_raw, krope_raw, pos, ckv_past, krope_past, q_norm, w_uq, kv_norm, w_ukv):
    B, L, _ = q_lat.shape
    q = (rmsnorm(q_lat, q_norm) @ w_uq).reshape(B, L, MLA_HEADS, MLA_NOPE_DIM + MLA_ROPE_DIM)
    q_nope = q[..., :MLA_NOPE_DIM]
    q_rope = rope(q[..., MLA_NOPE_DIM:], pos)
    ckv = rmsnorm(ckv_raw, kv_norm)
    krope = rope(krope_raw[:, :, None, :], pos)[:, :, 0]
    if ckv_past is None:
        ckv_all, krope_all, kpos = ckv, krope, pos
    else:
        P = ckv_past.shape[1]
        ckv_all = jnp.concatenate([ckv_past, ckv], axis=1)
        krope_all = jnp.concatenate([krope_past, krope], axis=1)
        kpos = jnp.concatenate([jnp.arange(P), pos])
    kv = jnp.einsum('bkc,chd->bkhd', ckv_all,
                    w_ukv.reshape(MLA_KV_RANK, MLA_HEADS, MLA_NOPE_DIM + MLA_V_DIM))
    k_nope = kv[..., :MLA_NOPE_DIM]
    v = kv[..., MLA_NOPE_DIM:]
    if L > Q_BLOCK and L % Q_BLOCK == 0:
        nb = L // Q_BLOCK

        def to_blocks(t):
            return jnp.moveaxis(t.reshape(B, nb, Q_BLOCK, *t.shape[2:]), 1, 0)

        o = lax.map(lambda a: mla_attend(a[0], a[1], a[2], k_nope, krope_all, v, kpos),
                    (to_blocks(q_nope), to_blocks(q_rope), pos.reshape(nb, Q_BLOCK)))
        o = jnp.moveaxis(o, 0, 1).reshape(B, L, D_MLA)
    else:
        o = mla_attend(q_nope, q_rope, pos, k_nope, krope_all, v, kpos).reshape(B, L, D_MLA)
    return o, ckv, krope


def layer(x, pos, pool_hist, ckv_past, krope_past, p):
    h = rmsnorm(x, p['g_pre_mix'])
    proj = h @ p['w_in']
    u = jax.nn.gelu(proj[..., OFF_U:OFF_V])
    v = jax.nn.gelu(proj[..., OFF_V:OFF_POOL])
    a, v_rows = chunk_mlp(u, v, p['cm_ln_g'], p['cm_ln_b'], p['cm_ws'], p['cm_bs'])
    b, pool_state = pool_mix(proj[..., OFF_POOL:OFF_Q], pool_hist, pos, p['pool_w'], p['pool_scale'])
    c, ckv, krope = mla(proj[..., OFF_Q:OFF_CKV], proj[..., OFF_CKV:OFF_KR], proj[..., OFF_KR:D_PROJ],
                        pos, ckv_past, krope_past, p['mla_q_norm'], p['mla_w_uq'],
                        p['mla_kv_norm'], p['mla_w_ukv'])
    mix = jnp.concatenate([a, b, c], axis=-1) @ p['w_out']
    x = x + rmsnorm(mix, p['g_post_mix'])
    h2 = rmsnorm(x, p['g_pre_ffn'])
    f = (jax.nn.silu(h2 @ p['ffn_w_gate']) * (h2 @ p['ffn_w_up'])) @ p['ffn_w_down']
    x = x + rmsnorm(f, p['g_post_ffn'])
    return x, v_rows, pool_state, ckv, krope


def setup_inputs(seed: int = 0) -> dict:
    key = jax.random.key(seed)
    ks = jax.random.split(key, 32)

    def nrm(k, shape, scale=1.0):
        return jax.random.normal(k, shape, dtype=jnp.float32) * scale

    def gain(k, shape):
        return 1.0 + nrm(k, shape, 0.05)

    return {
        'x_prompt': nrm(ks[0], (BATCH, SEQ, D_MODEL)),
        'x_sample': nrm(ks[1], (DEC_BATCH, DEC_SEQ, D_MODEL)),
        'cache_mla_ckv': nrm(ks[2], (DEPTH, DEC_BATCH, PAST_LEN, MLA_KV_RANK)),
        'cache_mla_krope': nrm(ks[3], (DEPTH, DEC_BATCH, PAST_LEN, MLA_ROPE_DIM)),
        'state_pool': nrm(ks[4], (DEPTH, DEC_BATCH, POOL_HIST, D_POOL)),
        'g_pre_mix': gain(ks[5], (DEPTH, D_MODEL)),
        'g_post_mix': gain(ks[6], (DEPTH, D_MODEL)),
        'g_pre_ffn': gain(ks[7], (DEPTH, D_MODEL)),
        'g_post_ffn': gain(ks[8], (DEPTH, D_MODEL)),
        'w_in': nrm(ks[9], (DEPTH, D_MODEL, D_PROJ), D_MODEL ** -0.5),
        'cm_ln_g': gain(ks[10], (DEPTH, D_CM)),
        'cm_ln_b': nrm(ks[11], (DEPTH, D_CM), 0.02),
        'cm_ws': nrm(ks[12], (DEPTH, CM_HEADS, CM_CHUNK, CM_CHUNK), 0.5 * CM_CHUNK ** -0.5),
        'cm_bs': 1.0 + nrm(ks[13], (DEPTH, CM_HEADS, CM_CHUNK), 0.1),
        'pool_w': nrm(ks[14], (DEPTH, POOL_GROUPS, POOL_GROUP_DIM, POOL_GROUP_DIM), POOL_GROUP_DIM ** -0.5),
        'pool_scale': gain(ks[15], (DEPTH, D_POOL)),
        'mla_q_norm': gain(ks[16], (DEPTH, MLA_Q_RANK)),
        'mla_w_uq': nrm(ks[17], (DEPTH, MLA_Q_RANK, MLA_HEADS * (MLA_NOPE_DIM + MLA_ROPE_DIM)), MLA_Q_RANK ** -0.5),
        'mla_kv_norm': gain(ks[18], (DEPTH, MLA_KV_RANK)),
        'mla_w_ukv': nrm(ks[19], (DEPTH, MLA_KV_RANK, MLA_HEADS * (MLA_NOPE_DIM + MLA_V_DIM)), MLA_KV_RANK ** -0.5),
        'w_out': nrm(ks[20], (DEPTH, D_MIX, D_MODEL), D_MIX ** -0.5),
        'ffn_w_gate': nrm(ks[21], (DEPTH, D_MODEL, D_FF), D_MODEL ** -0.5),
        'ffn_w_up': nrm(ks[22], (DEPTH, D_MODEL, D_FF), D_MODEL ** -0.5),
        'ffn_w_down': nrm(ks[23], (DEPTH, D_FF, D_MODEL), D_FF ** -0.5),
    }


def reference(x_prompt, x_sample, cache_mla_ckv, cache_mla_krope, state_pool,
              g_pre_mix, g_post_mix, g_pre_ffn, g_post_ffn, w_in, cm_ln_g, cm_ln_b, cm_ws, cm_bs,
              pool_w, pool_scale, mla_q_norm, mla_w_uq, mla_kv_norm, mla_w_ukv, w_out,
              ffn_w_gate, ffn_w_up, ffn_w_down):
    Bp, Lp, _ = x_prompt.shape
    Ls = x_sample.shape[1]
    P = cache_mla_ckv.shape[2]
    pos_p = jnp.arange(Lp)
    pos_s = P + jnp.arange(Ls)
    xp, xs = x_prompt, x_sample
    ckv_p, kr_p, pool_p, ckv_s, kr_s, pool_s, cmv_s = [], [], [], [], [], [], []
    for l in range(DEPTH):
        p = {
            'g_pre_mix': g_pre_mix[l], 'g_post_mix': g_post_mix[l],
            'g_pre_ffn': g_pre_ffn[l], 'g_post_ffn': g_post_ffn[l],
            'w_in': w_in[l], 'cm_ln_g': cm_ln_g[l], 'cm_ln_b': cm_ln_b[l],
            'cm_ws': cm_ws[l], 'cm_bs': cm_bs[l], 'pool_w': pool_w[l], 'pool_scale': pool_scale[l],
            'mla_q_norm': mla_q_norm[l], 'mla_w_uq': mla_w_uq[l],
            'mla_kv_norm': mla_kv_norm[l], 'mla_w_ukv': mla_w_ukv[l], 'w_out': w_out[l],
            'ffn_w_gate': ffn_w_gate[l], 'ffn_w_up': ffn_w_up[l], 'ffn_w_down': ffn_w_down[l],
        }
        zero_hist = jnp.zeros((Bp, POOL_HIST, D_POOL), xp.dtype)
        xp, _, pst, ckv, kr = layer(xp, pos_p, zero_hist, None, None, p)
        ckv_p.append(ckv)
        kr_p.append(kr)
        pool_p.append(pst)
        xs, vrows, pst, ckv, kr = layer(xs, pos_s, state_pool[l], cache_mla_ckv[l], cache_mla_krope[l], p)
        ckv_s.append(ckv)
        kr_s.append(kr)
        pool_s.append(pst)
        cmv_s.append(vrows)
    return (xp, xs, jnp.stack(ckv_p), jnp.stack(kr_p), jnp.stack(pool_p),
            jnp.stack(ckv_s), jnp.stack(kr_s), jnp.stack(pool_s), jnp.stack(cmv_s))
```

```python
import functools

import jax
import jax.numpy as jnp
from jax import lax
from jax.experimental import pallas as pl
from jax.experimental.pallas import tpu as pltpu

D_MODEL = 4096
DEPTH = 4
CHUNK = 64
EPS = 1e-6
CM_CHUNK = 128
CM_HEADS = 8
CM_HEAD_DIM = 128
D_CM = CM_HEADS * CM_HEAD_DIM
POOL_WINDOWS = (2, 4, 8, 16)
POOL_GROUP_DIM = 256
D_POOL = len(POOL_WINDOWS) * POOL_GROUP_DIM
POOL_HIST = max(POOL_WINDOWS) - 1
HIST_ROWS = POOL_HIST + 1
MLA_HEADS = 16
MLA_Q_RANK = 1024
MLA_KV_RANK = 512
MLA_NOPE_DIM = 128
MLA_ROPE_DIM = 64
MLA_V_DIM = 128
D_MLA = MLA_HEADS * MLA_V_DIM
MLA_SCALE = (MLA_NOPE_DIM + MLA_ROPE_DIM) ** -0.5
ROPE_THETA = 10000.0
Q_HEAD_PAD = 256
OFF_V = D_CM
OFF_POOL = 2 * D_CM
OFF_Q = OFF_POOL + D_POOL
OFF_CKV = OFF_Q + MLA_Q_RANK
OFF_KR = OFF_CKV + MLA_KV_RANK
D_FF = 11008
D_FF_PAD = 11264

LANES = 128
VMEM_LIMIT = 56 * 1024 * 1024

BF16 = jnp.bfloat16
F32 = jnp.float32


def _params(sem):
    return pltpu.CompilerParams(dimension_semantics=sem, vmem_limit_bytes=VMEM_LIMIT)


def _rms(x, g):
    r = lax.rsqrt(jnp.mean(x * x, axis=-1, keepdims=True) + EPS)
    return (x * r) * g


def _dot(a, b):
    return jnp.dot(a, b, preferred_element_type=F32)


def _dot_nt(a, b):
    return lax.dot_general(a, b, (((1,), (1,)), ((), ())), preferred_element_type=F32)


def _rope_lanes(b, c, s):
    return b * c + pltpu.roll(b, 64, axis=1) * s


def _prenorm_kernel(x_ref, g_ref, h_ref):
    h_ref[...] = _rms(x_ref[...], g_ref[...]).astype(BF16)


def prenorm(x, g, tm=512):
    m, d = x.shape
    return pl.pallas_call(
        _prenorm_kernel,
        grid=(m // tm,),
        in_specs=[pl.BlockSpec((tm, d), lambda i: (i, 0)), pl.BlockSpec((1, d), lambda i: (0, 0))],
        out_specs=pl.BlockSpec((tm, d), lambda i: (i, 0)),
        out_shape=jax.ShapeDtypeStruct((m, d), BF16),
        compiler_params=_params(("parallel",)),
        name="prenorm",
    )(x, g.reshape(1, d))


def _resnorm_kernel(x_ref, f_ref, gpost_ref, gpre_ref, xo_ref, h_ref):
    xn = x_ref[...] + _rms(f_ref[...], gpost_ref[...])
    xo_ref[...] = xn
    h_ref[...] = _rms(xn, gpre_ref[...]).astype(BF16)


def _resnorm_last_kernel(x_ref, f_ref, gpost_ref, xo_ref):
    xo_ref[...] = x_ref[...] + _rms(f_ref[...], gpost_ref[...])


def resnorm(x, f, g_post, g_pre, tm=256):
    m, d = x.shape
    row = pl.BlockSpec((tm, d), lambda i: (i, 0))
    vec = pl.BlockSpec((1, d), lambda i: (0, 0))
    if g_pre is None:
        return pl.pallas_call(
            _resnorm_last_kernel, grid=(m // tm,), in_specs=[row, row, vec], out_specs=row,
            out_shape=jax.ShapeDtypeStruct((m, d), F32), compiler_params=_params(("parallel",)),
            name="resnorm_last",
        )(x, f, g_post.reshape(1, d)), None
    return pl.pallas_call(
        _resnorm_kernel, grid=(m // tm,), in_specs=[row, row, vec, vec], out_specs=[row, row],
        out_shape=[jax.ShapeDtypeStruct((m, d), F32), jax.ShapeDtypeStruct((m, d), BF16)],
        compiler_params=_params(("parallel",)), name="resnorm",
    )(x, f, g_post.reshape(1, d), g_pre.reshape(1, d))


def _latnorm_kernel(q_ref, c_ref, gq_ref, gc_ref, qn_ref, cn_ref, cb_ref):
    qn_ref[...] = _rms(q_ref[...], gq_ref[...]).astype(BF16)
    cn = _rms(c_ref[...], gc_ref[...])
    cn_ref[...] = cn
    cb_ref[...] = cn.astype(BF16)


def latnorm(rest, gq, gc, tm=512):
    m = rest.shape[0]
    qb = (OFF_Q - OFF_POOL) // MLA_Q_RANK
    cb = (OFF_CKV - OFF_POOL) // MLA_KV_RANK
    return pl.pallas_call(
        _latnorm_kernel,
        grid=(m // tm,),
        in_specs=[pl.BlockSpec((tm, MLA_Q_RANK), lambda i: (i, qb)),
                  pl.BlockSpec((tm, MLA_KV_RANK), lambda i: (i, cb)),
                  pl.BlockSpec((1, MLA_Q_RANK), lambda i: (0, 0)),
                  pl.BlockSpec((1, MLA_KV_RANK), lambda i: (0, 0))],
        out_specs=[pl.BlockSpec((tm, MLA_Q_RANK), lambda i: (i, 0)),
                   pl.BlockSpec((tm, MLA_KV_RANK), lambda i: (i, 0)),
                   pl.BlockSpec((tm, MLA_KV_RANK), lambda i: (i, 0))],
        out_shape=[jax.ShapeDtypeStruct((m, MLA_Q_RANK), BF16),
                   jax.ShapeDtypeStruct((m, MLA_KV_RANK), F32),
                   jax.ShapeDtypeStruct((m, MLA_KV_RANK), BF16)],
        compiler_params=_params(("parallel",)), name="latnorm",
    )(rest, rest, gq.reshape(1, -1), gc.reshape(1, -1))


def _mm_kernel(a_ref, w_ref, o_ref, *, act):
    r = _dot(a_ref[...].astype(BF16), w_ref[...])
    if act == "gelu":
        r = jax.nn.gelu(r)
    o_ref[...] = r.astype(o_ref.dtype)


def matmul(a, w, *, n_off=0, n=None, act=None, out_dtype=F32, tm=1024, tn=512, name="mm"):
    m, k = a.shape
    n = w.shape[1] if n is None else n
    jo = n_off // tn
    return pl.pallas_call(
        functools.partial(_mm_kernel, act=act),
        grid=(m // tm, n // tn),
        in_specs=[pl.BlockSpec((tm, k), lambda i, j: (i, 0)),
                  pl.BlockSpec((k, tn), lambda i, j: (0, j + jo))],
        out_specs=pl.BlockSpec((tm, tn), lambda i, j: (i, j)),
        out_shape=jax.ShapeDtypeStruct((m, n), out_dtype),
        compiler_params=_params(("parallel", "arbitrary")), name=name,
    )(a, w)


def _mm_rope_kernel(a_ref, w_ref, c_ref, s_ref, o_ref, *, head_w):
    r = _dot(a_ref[...], w_ref[...])
    c = c_ref[...]
    s = s_ref[...]
    for h in range(r.shape[1] // head_w):
        lo = h * head_w
        if head_w > LANES:
            o_ref[:, lo:lo + head_w - LANES] = r[:, lo:lo + head_w - LANES].astype(o_ref.dtype)
        b = r[:, lo + head_w - LANES:lo + head_w]
        o_ref[:, lo + head_w - LANES:lo + head_w] = _rope_lanes(b, c, s).astype(o_ref.dtype)


def matmul_rope(a, w, cos_t, sin_t, *, head_w, out_dtype, tm=1024, tn=512, name="mm_rope"):
    m, k = a.shape
    n = w.shape[1]
    return pl.pallas_call(
        functools.partial(_mm_rope_kernel, head_w=head_w),
        grid=(m // tm, n // tn),
        in_specs=[pl.BlockSpec((tm, k), lambda i, j: (i, 0)),
                  pl.BlockSpec((k, tn), lambda i, j: (0, j)),
                  pl.BlockSpec((tm, LANES), lambda i, j: (i, 0)),
                  pl.BlockSpec((tm, LANES), lambda i, j: (i, 0))],
        out_specs=pl.BlockSpec((tm, tn), lambda i, j: (i, j)),
        out_shape=jax.ShapeDtypeStruct((m, n), out_dtype),
        compiler_params=_params(("parallel", "arbitrary")), name=name,
    )(a, w, cos_t, sin_t)


def _mix_out_kernel(a_ref, b_ref, c_ref, wa_ref, wb_ref, wc_ref, o_ref):
    o_ref[...] = _dot(a_ref[...], wa_ref[...]) + _dot(b_ref[...], wb_ref[...]) + _dot(c_ref[...], wc_ref[...])


def mix_out(a, b, c, w_out, tm=1024, tn=512):
    m = a.shape[0]
    n = w_out.shape[1]
    ka, kb, kc = a.shape[1], b.shape[1], c.shape[1]
    return pl.pallas_call(
        _mix_out_kernel,
        grid=(m // tm, n // tn),
        in_specs=[pl.BlockSpec((tm, ka), lambda i, j: (i, 0)),
                  pl.BlockSpec((tm, kb), lambda i, j: (i, 0)),
                  pl.BlockSpec((tm, kc), lambda i, j: (i, 0)),
                  pl.BlockSpec((ka, tn), lambda i, j: (0, j)),
                  pl.BlockSpec((kb, tn), lambda i, j: (ka // kb, j)),
                  pl.BlockSpec((kc, tn), lambda i, j: ((ka + kb) // kc, j))],
        out_specs=pl.BlockSpec((tm, tn), lambda i, j: (i, j)),
        out_shape=jax.ShapeDtypeStruct((m, n), F32),
        compiler_params=_params(("parallel", "arbitrary")), name="mix_out",
    )(a, b, c, w_out, w_out, w_out)


def _ffn_up_kernel(h_ref, wg_ref, wu_ref, o_ref):
    h = h_ref[...]
    g = _dot(h, wg_ref[...])
    u = _dot(h, wu_ref[...])
    o_ref[...] = (jax.nn.silu(g) * u).astype(o_ref.dtype)


def ffn_up(h, wg, wu, tm=1024, tn=512):
    m, k = h.shape
    n = wg.shape[1]
    return pl.pallas_call(
        _ffn_up_kernel,
        grid=(m // tm, n // tn),
        in_specs=[pl.BlockSpec((tm, k), lambda i, j: (i, 0)),
                  pl.BlockSpec((k, tn), lambda i, j: (0, j)),
                  pl.BlockSpec((k, tn), lambda i, j: (0, j))],
        out_specs=pl.BlockSpec((tm, tn), lambda i, j: (i, j)),
        out_shape=jax.ShapeDtypeStruct((m, n), BF16),
        compiler_params=_params(("parallel", "arbitrary")), name="ffn_up",
    )(h, wg, wu)


def _ffn_down_kernel(a_ref, w_ref, o_ref):
    k = pl.program_id(1)
    r = _dot(a_ref[...], w_ref[...])

    @pl.when(k == 0)
    def _():
        o_ref[...] = r

    @pl.when(k > 0)
    def _():
        o_ref[...] += r


def ffn_down(a, w, tm=512, tk=512):
    m, kk = a.shape
    n = w.shape[1]
    return pl.pallas_call(
        _ffn_down_kernel,
        grid=(m // tm, kk // tk),
        in_specs=[pl.BlockSpec((tm, tk), lambda i, k: (i, k)),
                  pl.BlockSpec((tk, n), lambda i, k: (k, 0))],
        out_specs=pl.BlockSpec((tm, n), lambda i, k: (i, 0)),
        out_shape=jax.ShapeDtypeStruct((m, n), F32),
        compiler_params=_params(("parallel", "arbitrary")), name="ffn_down",
    )(a, w)


def _chunk_mlp_kernel(u_ref, v_ref, g_ref, b_ref, w_ref, bs_ref, a_ref, vn_ref, *, first_sample_block):
    is_sample = pl.program_id(0) >= first_sample_block
    v = v_ref[...]
    mu = jnp.mean(v, axis=-1, keepdims=True)
    d = v - mu
    var = jnp.mean(d * d, axis=-1, keepdims=True)
    vn = (d * lax.rsqrt(var + EPS)) * g_ref[...] + b_ref[...]
    vn_ref[...] = vn
    vb = vn.astype(BF16)
    ri = lax.broadcasted_iota(jnp.int32, (CM_CHUNK, CM_CHUNK), 0) // CHUNK
    ci = lax.broadcasted_iota(jnp.int32, (CM_CHUNK, CM_CHUNK), 1) // CHUNK
    max_lag = jnp.where(is_sample, 0, 1)
    keep = (ri - ci >= 0) & (ri - ci <= max_lag)
    bs = bs_ref[0]
    for h in range(CM_HEADS):
        cols = slice(h * CM_HEAD_DIM, (h + 1) * CM_HEAD_DIM)
        w = jnp.where(keep, w_ref[0, h], 0.0).astype(BF16)
        sv = _dot(w, vb[:, cols]) + bs[:, h:h + 1]
        a_ref[:, cols] = (u_ref[:, cols] * sv).astype(a_ref.dtype)


def chunk_mlp(uv, ln_g, ln_b, ws2, bs2, first_sample_block):
    m = uv.shape[0]
    sel = lambda c: jnp.where(c >= first_sample_block, 1, 0)
    return pl.pallas_call(
        functools.partial(_chunk_mlp_kernel, first_sample_block=first_sample_block),
        grid=(m // CM_CHUNK,),
        in_specs=[pl.BlockSpec((CM_CHUNK, D_CM), lambda c: (c, 0)),
                  pl.BlockSpec((CM_CHUNK, D_CM), lambda c: (c, 1)),
                  pl.BlockSpec((1, D_CM), lambda c: (0, 0)),
                  pl.BlockSpec((1, D_CM), lambda c: (0, 0)),
                  pl.BlockSpec((1, CM_HEADS, CM_CHUNK, CM_CHUNK), lambda c: (sel(c), 0, 0, 0)),
                  pl.BlockSpec((1, CM_CHUNK, CM_HEADS), lambda c: (sel(c), 0, 0))],
        out_specs=[pl.BlockSpec((CM_CHUNK, D_CM), lambda c: (c, 0)),
                   pl.BlockSpec((CM_CHUNK, D_CM), lambda c: (c, 0))],
        out_shape=[jax.ShapeDtypeStruct((m, D_CM), BF16), jax.ShapeDtypeStruct((m, D_CM), F32)],
        compiler_params=_params(("parallel",)), name="chunk_mlp",
    )(uv, uv, ln_g.reshape(1, -1), ln_b.reshape(1, -1), ws2, bs2)


def _pool_kernel(*refs, t_rows, pos0, aliased):
    if aliased:
        x_ref, hist_ref, w_ref, sc_ref, _, o_ref, e_ref = refs
    else:
        x_ref, hist_ref, w_ref, sc_ref, o_ref, e_ref = refs
    t = pl.program_id(1)

    @pl.when(t == 0)
    def _():
        e_ref[0:HIST_ROWS, :] = hist_ref[0]

    x = x_ref[...]
    e_ref[HIST_ROWS:HIST_ROWS + t_rows, :] = x
    pos = pos0 + t * t_rows + lax.broadcasted_iota(jnp.int32, (t_rows, 1), 0)
    for g, w in enumerate(POOL_WINDOWS):
        cols = slice(g * POOL_GROUP_DIM, (g + 1) * POOL_GROUP_DIM)
        acc = x[:, cols]
        for k in range(1, w):
            acc = acc + e_ref[HIST_ROWS - k:HIST_ROWS - k + t_rows, cols]
        cnt = jnp.minimum(w, pos + 1).astype(F32)
        mg = (acc / cnt - x[:, cols]).astype(BF16)
        y = _dot(mg, w_ref[g]) * sc_ref[:, cols]
        o_ref[:, cols] = y.astype(o_ref.dtype)
    e_ref[0:HIST_ROWS, :] = e_ref[t_rows:t_rows + HIST_ROWS, :]


def pool_mix(rest, hist, pool_w, pool_scale, *, row0, n_seq, seq_len, t_rows, pos0, m_total, prev=None):
    n_t = seq_len // t_rows
    b0 = row0 // t_rows
    row_map = lambda s, t: (b0 + s * n_t + t, 0)
    in_specs = [pl.BlockSpec((t_rows, D_POOL), row_map),
                pl.BlockSpec((1, HIST_ROWS, D_POOL), lambda s, t: (s, 0, 0)),
                pl.BlockSpec(pool_w.shape, lambda s, t: (0, 0, 0)),
                pl.BlockSpec((1, D_POOL), lambda s, t: (0, 0))]
    args = [rest, hist, pool_w, pool_scale.reshape(1, -1)]
    aliases = {}
    if prev is not None:
        in_specs.append(pl.BlockSpec(memory_space=pl.ANY))
        args.append(prev)
        aliases = {4: 0}
    return pl.pallas_call(
        functools.partial(_pool_kernel, t_rows=t_rows, pos0=pos0, aliased=prev is not None),
        grid=(n_seq, n_t),
        in_specs=in_specs,
        out_specs=pl.BlockSpec((t_rows, D_POOL), row_map),
        out_shape=jax.ShapeDtypeStruct((m_total, D_POOL), BF16),
        scratch_shapes=[pltpu.VMEM((HIST_ROWS + t_rows, D_POOL), F32)],
        input_output_aliases=aliases,
        compiler_params=_params(("parallel", "arbitrary")), name="pool_mix",
    )(*args)


def _attn_kernel(*refs, tq, tk, causal, aliased):
    if aliased:
        q_ref, kn_ref, kr_ref, v_ref, _, o_ref, m_ref, l_ref, acc_ref = refs
    else:
        q_ref, kn_ref, kr_ref, v_ref, o_ref, m_ref, l_ref, acc_ref = refs
    qi = pl.program_id(2)
    ki = pl.program_id(3)

    @pl.when(ki == 0)
    def _():
        m_ref[...] = jnp.full_like(m_ref, -jnp.inf)
        l_ref[...] = jnp.zeros_like(l_ref)
        acc_ref[...] = jnp.zeros_like(acc_ref)

    def step():
        q = q_ref[...]
        s = _dot_nt(q[:, :MLA_NOPE_DIM], kn_ref[...]) + _dot_nt(q[:, MLA_NOPE_DIM:], kr_ref[...])
        s = s * MLA_SCALE
        if causal:
            qc = (qi * tq + lax.broadcasted_iota(jnp.int32, (tq, tk), 0)) // CHUNK
            kc = (ki * tk + lax.broadcasted_iota(jnp.int32, (tq, tk), 1)) // CHUNK
            s = jnp.where(kc <= qc, s, -jnp.inf)
        m_prev = m_ref[...]
        m_new = jnp.maximum(m_prev, jnp.max(s, axis=-1, keepdims=True))
        alpha = jnp.exp(m_prev - m_new)
        p = jnp.exp(s - m_new)
        l_ref[...] = alpha * l_ref[...] + jnp.sum(p, axis=-1, keepdims=True)
        acc_ref[...] = alpha * acc_ref[...] + _dot(p.astype(BF16), v_ref[...])
        m_ref[...] = m_new

    if causal:
        pl.when(ki <= qi)(step)
    else:
        step()

    @pl.when(ki == pl.num_programs(3) - 1)
    def _():
        o_ref[...] = (acc_ref[...] / l_ref[...]).astype(o_ref.dtype)


def attention(q, kv, kr, *, q_row0, n_batch, q_len, k_len, tq, tk, causal, m_total, prev=None):
    nq = q_len // tq
    nk = k_len // tk
    qb0 = q_row0 // tq
    if causal:
        kmap = lambda b, h, qi, ki: (b * nk + jnp.minimum(ki, qi), h)
        krmap = lambda b, h, qi, ki: (b * nk + jnp.minimum(ki, qi), 0)
        vmap_ = lambda b, h, qi, ki: (b * nk + jnp.minimum(ki, qi), MLA_HEADS + h)
    else:
        kmap = lambda b, h, qi, ki: (b * nk + ki, h)
        krmap = lambda b, h, qi, ki: (b * nk + ki, 0)
        vmap_ = lambda b, h, qi, ki: (b * nk + ki, MLA_HEADS + h)
    qmap = lambda b, h, qi, ki: (qb0 + b * nq + qi, h)
    in_specs = [pl.BlockSpec((tq, Q_HEAD_PAD), qmap),
                pl.BlockSpec((tk, MLA_NOPE_DIM), kmap),
                pl.BlockSpec((tk, LANES), krmap),
                pl.BlockSpec((tk, MLA_V_DIM), vmap_)]
    args = [q, kv, kr, kv]
    aliases = {}
    if prev is not None:
        in_specs.append(pl.BlockSpec(memory_space=pl.ANY))
        args.append(prev)
        aliases = {4: 0}
    return pl.pallas_call(
        functools.partial(_attn_kernel, tq=tq, tk=tk, causal=causal, aliased=prev is not None),
        grid=(n_batch, MLA_HEADS, nq, nk),
        in_specs=in_specs,
        out_specs=pl.BlockSpec((tq, MLA_V_DIM), qmap),
        out_shape=jax.ShapeDtypeStruct((m_total, D_MLA), BF16),
        scratch_shapes=[pltpu.VMEM((tq, 1), F32), pltpu.VMEM((tq, 1), F32), pltpu.VMEM((tq, MLA_V_DIM), F32)],
        input_output_aliases=aliases,
        compiler_params=_params(("parallel", "parallel", "parallel", "arbitrary")), name="attention",
    )(*args)


def _rope_tables(pos):
    half = MLA_ROPE_DIM // 2
    inv = ROPE_THETA ** (-jnp.arange(half, dtype=F32) / half)
    ang = pos.astype(F32)[:, None] * inv[None, :]
    cos, sin = jnp.cos(ang), jnp.sin(ang)
    z = jnp.zeros((pos.shape[0], LANES - MLA_ROPE_DIM), F32)
    return jnp.concatenate([cos, cos, z], axis=1), jnp.concatenate([-sin, sin, z], axis=1)


def _swap_halves(w):
    half = w.shape[-1] // 2
    return jnp.concatenate([w[..., half:], w[..., :half]], axis=-1)


def kernel(x_prompt, x_sample, cache_mla_ckv, cache_mla_krope, state_pool, g_pre_mix, g_post_mix, g_pre_ffn, g_post_ffn, w_in, cm_ln_g, cm_ln_b, cm_ws, cm_bs, pool_w, pool_scale, mla_q_norm, mla_w_uq, mla_kv_norm, mla_w_ukv, w_out, ffn_w_gate, ffn_w_up, ffn_w_down):
    bp, lp, d = x_prompt.shape
    bs_, ls, _ = x_sample.shape
    past = cache_mla_ckv.shape[2]
    mp, ms = bp * lp, bs_ * ls
    m = mp + ms

    x = jnp.concatenate([x_prompt.reshape(mp, d), x_sample.reshape(ms, d)], axis=0)
    pos = jnp.concatenate([jnp.tile(jnp.arange(lp), bp), jnp.tile(past + jnp.arange(ls), bs_)])
    cos_t, sin_t = _rope_tables(pos)
    pos_past = jnp.tile(jnp.arange(past), bs_)

    w_in_b = w_in.astype(BF16)
    w_kr = w_in[:, :, OFF_KR:OFF_KR + MLA_ROPE_DIM]
    w_kr2 = jnp.concatenate([w_kr, _swap_halves(w_kr)], axis=-1).astype(BF16)
    wq = mla_w_uq.reshape(DEPTH, MLA_Q_RANK, MLA_HEADS, MLA_NOPE_DIM + MLA_ROPE_DIM)
    wq_rope = wq[..., MLA_NOPE_DIM:]
    wq2 = jnp.concatenate([wq[..., :MLA_NOPE_DIM], wq_rope, _swap_halves(wq_rope)], axis=-1)
    wq2 = wq2.reshape(DEPTH, MLA_Q_RANK, MLA_HEADS * Q_HEAD_PAD).astype(BF16)
    wkv = mla_w_ukv.reshape(DEPTH, MLA_KV_RANK, MLA_HEADS, MLA_NOPE_DIM + MLA_V_DIM)
    wkv2 = jnp.concatenate([wkv[..., :MLA_NOPE_DIM].reshape(DEPTH, MLA_KV_RANK, -1),
                            wkv[..., MLA_NOPE_DIM:].reshape(DEPTH, MLA_KV_RANK, -1)], axis=-1).astype(BF16)
    w_out_b = w_out.astype(BF16)
    pad_ff = D_FF_PAD - D_FF
    wg_b = jnp.pad(ffn_w_gate.astype(BF16), ((0, 0), (0, 0), (0, pad_ff)))
    wu_b = jnp.pad(ffn_w_up.astype(BF16), ((0, 0), (0, 0), (0, pad_ff)))
    wd_b = jnp.pad(ffn_w_down.astype(BF16), ((0, 0), (0, pad_ff), (0, 0)))
    pool_w_b = pool_w.astype(BF16)
    ws_s = cm_ws.at[:, :, CHUNK:, CHUNK:].set(cm_ws[:, :, :CHUNK, :CHUNK])
    ws2 = jnp.stack([cm_ws, ws_s], axis=1)
    bs_s = jnp.concatenate([cm_bs[:, :, :CHUNK], cm_bs[:, :, :CHUNK]], axis=-1)
    bs2 = jnp.swapaxes(jnp.stack([cm_bs, bs_s], axis=1), -1, -2)
    hist_p = jnp.zeros((bp, HIST_ROWS, D_POOL), F32)
    hist_s = jnp.pad(state_pool, ((0, 0), (0, 0), (HIST_ROWS - POOL_HIST, 0), (0, 0)))
    krope_past = jnp.pad(cache_mla_krope, ((0, 0), (0, 0), (0, 0), (0, LANES - MLA_ROPE_DIM))).astype(BF16)

    outs = {k: [] for k in ("ckv_p", "kr_p", "pool_p", "ckv_s", "kr_s", "pool_s", "cmv_s")}
    h = prenorm(x, g_pre_mix[0])
    for l in range(DEPTH):
        uv = matmul(h, w_in_b[l], n_off=0, n=OFF_POOL, act="gelu", name="in_uv")
        rest = matmul(h, w_in_b[l], n_off=OFF_POOL, n=OFF_KR - OFF_POOL, name="in_rest")
        kr = matmul_rope(h, w_kr2[l], cos_t, sin_t, head_w=LANES, out_dtype=F32, tn=LANES, name="in_kr")
        a, vn = chunk_mlp(uv, cm_ln_g[l], cm_ln_b[l], ws2[l], bs2[l], mp // CM_CHUNK)
        b = pool_mix(rest, hist_p, pool_w_b[l], pool_scale[l], row0=0, n_seq=bp, seq_len=lp,
                     t_rows=256, pos0=0, m_total=m)
        b = pool_mix(rest, hist_s[l], pool_w_b[l], pool_scale[l], row0=mp, n_seq=bs_, seq_len=ls,
                     t_rows=ls, pos0=past, m_total=m, prev=b)
        qn, ckv, ckv_b = latnorm(rest, mla_q_norm[l], mla_kv_norm[l])
        q = matmul_rope(qn, wq2[l], cos_t, sin_t, head_w=Q_HEAD_PAD, out_dtype=BF16, name="q_up")
        kv_new = matmul(ckv_b, wkv2[l], out_dtype=BF16, name="kv_up")
        kv_past = matmul(cache_mla_ckv[l].reshape(bs_ * past, MLA_KV_RANK), wkv2[l], out_dtype=BF16,
                         name="kv_up_past")
        kr_b = kr.astype(BF16)
        c = attention(q, kv_new, kr_b, q_row0=0, n_batch=bp, q_len=lp, k_len=lp, tq=512, tk=512,
                      causal=True, m_total=m)
        kv_s = jnp.concatenate([kv_past.reshape(bs_, past, -1), kv_new[mp:].reshape(bs_, ls, -1)], axis=1)
        kr_s = jnp.concatenate([krope_past[l], kr_b[mp:].reshape(bs_, ls, LANES)], axis=1)
        k_len_s = past + ls
        c = attention(q, kv_s.reshape(bs_ * k_len_s, -1), kr_s.reshape(bs_ * k_len_s, LANES), q_row0=mp,
                      n_batch=bs_, q_len=ls, k_len=k_len_s, tq=ls, tk=k_len_s, causal=False, m_total=m,
                      prev=c)
        mix = mix_out(a, b, c, w_out_b[l])
        x, h2 = resnorm(x, mix, g_post_mix[l], g_pre_ffn[l])
        act = ffn_up(h2, wg_b[l], wu_b[l])
        f = ffn_down(act, wd_b[l])
        x, h = resnorm(x, f, g_post_ffn[l], g_pre_mix[l + 1] if l + 1 < DEPTH else None)

        pool_raw = rest[:, :D_POOL]
        outs["ckv_p"].append(ckv[:mp].reshape(bp, lp, MLA_KV_RANK))
        outs["ckv_s"].append(ckv[mp:].reshape(bs_, ls, MLA_KV_RANK))
        outs["kr_p"].append(kr[:mp, :MLA_ROPE_DIM].reshape(bp, lp, MLA_ROPE_DIM))
        outs["kr_s"].append(kr[mp:, :MLA_ROPE_DIM].reshape(bs_, ls, MLA_ROPE_DIM))
        outs["pool_p"].append(pool_raw[:mp].reshape(bp, lp, D_POOL)[:, lp - POOL_HIST:])
        outs["pool_s"].append(pool_raw[mp:].reshape(bs_, ls, D_POOL)[:, ls - POOL_HIST:])
        outs["cmv_s"].append(vn[mp:].reshape(bs_, ls, D_CM))

    return (x[:mp].reshape(bp, lp, d), x[mp:].reshape(bs_, ls, d),
            jnp.stack(outs["ckv_p"]), jnp.stack(outs["kr_p"]), jnp.stack(outs["pool_p"]),
            jnp.stack(outs["ckv_s"]), jnp.stack(outs["kr_s"]), jnp.stack(outs["pool_s"]),
            jnp.stack(outs["cmv_s"]))
```

```python
import functools

import jax
import jax.numpy as jnp
from jax import lax
from jax.experimental import pallas as pl
from jax.experimental.pallas import tpu as pltpu

D_MODEL = 4096
DEPTH = 4
CHUNK = 64
EPS = 1e-6
CM_CHUNK = 128
CM_HEADS = 8
CM_HEAD_DIM = 128
D_CM = CM_HEADS * CM_HEAD_DIM
POOL_WINDOWS = (2, 4, 8, 16)
POOL_GROUP_DIM = 256
D_POOL = len(POOL_WINDOWS) * POOL_GROUP_DIM
POOL_HIST = max(POOL_WINDOWS) - 1
HIST_ROWS = POOL_HIST + 1
MLA_HEADS = 16
MLA_Q_RANK = 1024
MLA_KV_RANK = 512
MLA_NOPE_DIM = 128
MLA_ROPE_DIM = 64
MLA_V_DIM = 128
D_MLA = MLA_HEADS * MLA_V_DIM
MLA_SCALE = (MLA_NOPE_DIM + MLA_ROPE_DIM) ** -0.5
ROPE_THETA = 10000.0
Q_HEAD_PAD = 256
OFF_V = D_CM
OFF_POOL = 2 * D_CM
OFF_Q = OFF_POOL + D_POOL
OFF_CKV = OFF_Q + MLA_Q_RANK
OFF_KR = OFF_CKV + MLA_KV_RANK
FF_TILE = 256

LANES = 128
VMEM_LIMIT = 56 * 1024 * 1024

BF16 = jnp.bfloat16
F32 = jnp.float32


def _params(sem):
    return pltpu.CompilerParams(dimension_semantics=sem, vmem_limit_bytes=VMEM_LIMIT)


def _rms(x, g):
    r = lax.rsqrt(jnp.mean(x * x, axis=-1, keepdims=True) + EPS)
    return (x * r) * g


def _dot(a, b):
    return jnp.dot(a, b, preferred_element_type=F32)


def _dot_nt(a, b):
    return lax.dot_general(a, b, (((1,), (1,)), ((), ())), preferred_element_type=F32)


def _rope_lanes(b, c, s):
    return b * c + pltpu.roll(b, 64, axis=1) * s


def _prenorm_kernel(x_ref, g_ref, h_ref):
    h_ref[...] = _rms(x_ref[...], g_ref[...]).astype(BF16)


def prenorm(x, g, tm=512):
    m, d = x.shape
    return pl.pallas_call(
        _prenorm_kernel,
        grid=(m // tm,),
        in_specs=[pl.BlockSpec((tm, d), lambda i: (i, 0)), pl.BlockSpec((1, d), lambda i: (0, 0))],
        out_specs=pl.BlockSpec((tm, d), lambda i: (i, 0)),
        out_shape=jax.ShapeDtypeStruct((m, d), BF16),
        compiler_params=_params(("parallel",)),
        name="prenorm",
    )(x, g.reshape(1, d))


def _resnorm_kernel(x_ref, f_ref, gpost_ref, gpre_ref, xo_ref, h_ref):
    xn = x_ref[...] + _rms(f_ref[...], gpost_ref[...])
    xo_ref[...] = xn
    h_ref[...] = _rms(xn, gpre_ref[...]).astype(BF16)


def _resnorm_last_kernel(x_ref, f_ref, gpost_ref, xo_ref):
    xo_ref[...] = x_ref[...] + _rms(f_ref[...], gpost_ref[...])


def resnorm(x, f, g_post, g_pre, tm=256):
    m, d = x.shape
    row = pl.BlockSpec((tm, d), lambda i: (i, 0))
    vec = pl.BlockSpec((1, d), lambda i: (0, 0))
    if g_pre is None:
        return pl.pallas_call(
            _resnorm_last_kernel, grid=(m // tm,), in_specs=[row, row, vec], out_specs=row,
            out_shape=jax.ShapeDtypeStruct((m, d), F32), compiler_params=_params(("parallel",)),
            name="resnorm_last",
        )(x, f, g_post.reshape(1, d)), None
    return pl.pallas_call(
        _resnorm_kernel, grid=(m // tm,), in_specs=[row, row, vec, vec], out_specs=[row, row],
        out_shape=[jax.ShapeDtypeStruct((m, d), F32), jax.ShapeDtypeStruct((m, d), BF16)],
        compiler_params=_params(("parallel",)), name="resnorm",
    )(x, f, g_post.reshape(1, d), g_pre.reshape(1, d))


def _latnorm_kernel(q_ref, c_ref, gq_ref, gc_ref, qn_ref, cn_ref, cb_ref):
    qn_ref[...] = _rms(q_ref[...], gq_ref[...]).astype(BF16)
    cn = _rms(c_ref[...], gc_ref[...])
    cn_ref[...] = cn
    cb_ref[...] = cn.astype(BF16)


def latnorm(rest, gq, gc, tm=512):
    m = rest.shape[0]
    qb = (OFF_Q - OFF_POOL) // MLA_Q_RANK
    cb = (OFF_CKV - OFF_POOL) // MLA_KV_RANK
    return pl.pallas_call(
        _latnorm_kernel,
        grid=(m // tm,),
        in_specs=[pl.BlockSpec((tm, MLA_Q_RANK), lambda i: (i, qb)),
                  pl.BlockSpec((tm, MLA_KV_RANK), lambda i: (i, cb)),
                  pl.BlockSpec((1, MLA_Q_RANK), lambda i: (0, 0)),
                  pl.BlockSpec((1, MLA_KV_RANK), lambda i: (0, 0))],
        out_specs=[pl.BlockSpec((tm, MLA_Q_RANK), lambda i: (i, 0)),
                   pl.BlockSpec((tm, MLA_KV_RANK), lambda i: (i, 0)),
                   pl.BlockSpec((tm, MLA_KV_RANK), lambda i: (i, 0))],
        out_shape=[jax.ShapeDtypeStruct((m, MLA_Q_RANK), BF16),
                   jax.ShapeDtypeStruct((m, MLA_KV_RANK), F32),
                   jax.ShapeDtypeStruct((m, MLA_KV_RANK), BF16)],
        compiler_params=_params(("parallel",)), name="latnorm",
    )(rest, rest, gq.reshape(1, -1), gc.reshape(1, -1))


def _mm_kernel(a_ref, w_ref, o_ref, *, act):
    r = _dot(a_ref[...], w_ref[...])
    if act == "gelu":
        r = jax.nn.gelu(r)
    elif act == "swiglu":
        half = r.shape[1] // 2
        r = jax.nn.silu(r[:, :half]) * r[:, half:]
    o_ref[...] = r.astype(o_ref.dtype)


def matmul(a, w, l, *, n_off=0, n=None, act=None, out_dtype=F32, tm=1024, tn=512, name="mm"):
    m, k = a.shape
    n = w.shape[2] if n is None else n
    jo = n_off // tn
    tn_out = tn // 2 if act == "swiglu" else tn
    n_out = n // 2 if act == "swiglu" else n
    return pl.pallas_call(
        functools.partial(_mm_kernel, act=act),
        grid=(m // tm, n // tn),
        in_specs=[pl.BlockSpec((tm, k), lambda i, j: (i, 0)),
                  pl.BlockSpec((None, k, tn), lambda i, j: (l, 0, j + jo))],
        out_specs=pl.BlockSpec((tm, tn_out), lambda i, j: (i, j)),
        out_shape=jax.ShapeDtypeStruct((m, n_out), out_dtype),
        compiler_params=_params(("parallel", "arbitrary")), name=name,
    )(a, w)


def _mm_rope_kernel(a_ref, w_ref, c_ref, s_ref, o_ref, *, head_w):
    r = _dot(a_ref[...], w_ref[...])
    c = c_ref[...]
    s = s_ref[...]
    for h in range(r.shape[1] // head_w):
        lo = h * head_w
        if head_w > LANES:
            o_ref[:, lo:lo + head_w - LANES] = r[:, lo:lo + head_w - LANES].astype(o_ref.dtype)
        b = r[:, lo + head_w - LANES:lo + head_w]
        o_ref[:, lo + head_w - LANES:lo + head_w] = _rope_lanes(b, c, s).astype(o_ref.dtype)


def matmul_rope(a, w, l, cos_t, sin_t, *, head_w, out_dtype, tm=1024, tn=512, name="mm_rope"):
    m, k = a.shape
    n = w.shape[2]
    return pl.pallas_call(
        functools.partial(_mm_rope_kernel, head_w=head_w),
        grid=(m // tm, n // tn),
        in_specs=[pl.BlockSpec((tm, k), lambda i, j: (i, 0)),
                  pl.BlockSpec((None, k, tn), lambda i, j: (l, 0, j)),
                  pl.BlockSpec((tm, LANES), lambda i, j: (i, 0)),
                  pl.BlockSpec((tm, LANES), lambda i, j: (i, 0))],
        out_specs=pl.BlockSpec((tm, tn), lambda i, j: (i, j)),
        out_shape=jax.ShapeDtypeStruct((m, n), out_dtype),
        compiler_params=_params(("parallel", "arbitrary")), name=name,
    )(a, w, cos_t, sin_t)


def _mix_out_kernel(a_ref, b_ref, c_ref, wa_ref, wb_ref, wc_ref, o_ref):
    o_ref[...] = _dot(a_ref[...], wa_ref[...]) + _dot(b_ref[...], wb_ref[...]) + _dot(c_ref[...], wc_ref[...])


def mix_out(a, b, c, w_out, l, tm=1024, tn=512):
    m = a.shape[0]
    n = w_out.shape[2]
    ka, kb, kc = a.shape[1], b.shape[1], c.shape[1]
    return pl.pallas_call(
        _mix_out_kernel,
        grid=(m // tm, n // tn),
        in_specs=[pl.BlockSpec((tm, ka), lambda i, j: (i, 0)),
                  pl.BlockSpec((tm, kb), lambda i, j: (i, 0)),
                  pl.BlockSpec((tm, kc), lambda i, j: (i, 0)),
                  pl.BlockSpec((None, ka, tn), lambda i, j: (l, 0, j)),
                  pl.BlockSpec((None, kb, tn), lambda i, j: (l, ka // kb, j)),
                  pl.BlockSpec((None, kc, tn), lambda i, j: (l, (ka + kb) // kc, j))],
        out_specs=pl.BlockSpec((tm, tn), lambda i, j: (i, j)),
        out_shape=jax.ShapeDtypeStruct((m, n), F32),
        compiler_params=_params(("parallel", "arbitrary")), name="mix_out",
    )(a, b, c, w_out, w_out, w_out)


def _chunk_mlp_kernel(u_ref, v_ref, g_ref, b_ref, w_ref, bs_ref, a_ref, vn_ref, *, first_sample_block):
    is_sample = pl.program_id(0) >= first_sample_block
    v = v_ref[...]
    mu = jnp.mean(v, axis=-1, keepdims=True)
    d = v - mu
    var = jnp.mean(d * d, axis=-1, keepdims=True)
    vn = (d * lax.rsqrt(var + EPS)) * g_ref[...] + b_ref[...]
    vn_ref[...] = vn
    vb = vn.astype(BF16)
    ri = lax.broadcasted_iota(jnp.int32, (CM_CHUNK, CM_CHUNK), 0) // CHUNK
    ci = lax.broadcasted_iota(jnp.int32, (CM_CHUNK, CM_CHUNK), 1) // CHUNK
    max_lag = jnp.where(is_sample, 0, 1)
    keep = (ri - ci >= 0) & (ri - ci <= max_lag)
    bs = bs_ref[0]
    for h in range(CM_HEADS):
        cols = slice(h * CM_HEAD_DIM, (h + 1) * CM_HEAD_DIM)
        w = jnp.where(keep, w_ref[0, h], 0.0).astype(BF16)
        sv = _dot(w, vb[:, cols]) + bs[:, h:h + 1]
        a_ref[:, cols] = (u_ref[:, cols] * sv).astype(a_ref.dtype)


def chunk_mlp(uv, ln_g, ln_b, ws2, bs2, l, first_sample_block):
    m = uv.shape[0]
    sel = lambda c: jnp.where(c >= first_sample_block, 1, 0)
    return pl.pallas_call(
        functools.partial(_chunk_mlp_kernel, first_sample_block=first_sample_block),
        grid=(m // CM_CHUNK,),
        in_specs=[pl.BlockSpec((CM_CHUNK, D_CM), lambda c: (c, 0)),
                  pl.BlockSpec((CM_CHUNK, D_CM), lambda c: (c, 1)),
                  pl.BlockSpec((1, D_CM), lambda c: (0, 0)),
                  pl.BlockSpec((1, D_CM), lambda c: (0, 0)),
                  pl.BlockSpec((None, 1, CM_HEADS, CM_CHUNK, CM_CHUNK), lambda c: (l, sel(c), 0, 0, 0)),
                  pl.BlockSpec((None, 1, CM_CHUNK, CM_HEADS), lambda c: (l, sel(c), 0, 0))],
        out_specs=[pl.BlockSpec((CM_CHUNK, D_CM), lambda c: (c, 0)),
                   pl.BlockSpec((CM_CHUNK, D_CM), lambda c: (c, 0))],
        out_shape=[jax.ShapeDtypeStruct((m, D_CM), BF16), jax.ShapeDtypeStruct((m, D_CM), F32)],
        compiler_params=_params(("parallel",)), name="chunk_mlp",
    )(uv, uv, ln_g.reshape(1, -1), ln_b.reshape(1, -1), ws2, bs2)


def _pool_kernel(*refs, t_rows, pos0, aliased):
    if aliased:
        x_ref, hist_ref, w_ref, sc_ref, _, o_ref, e_ref = refs
    else:
        x_ref, hist_ref, w_ref, sc_ref, o_ref, e_ref = refs
    t = pl.program_id(1)

    @pl.when(t == 0)
    def _():
        e_ref[0:HIST_ROWS, :] = hist_ref[0]

    x = x_ref[...]
    e_ref[HIST_ROWS:HIST_ROWS + t_rows, :] = x
    pos = pos0 + t * t_rows + lax.broadcasted_iota(jnp.int32, (t_rows, 1), 0)
    for g, w in enumerate(POOL_WINDOWS):
        cols = slice(g * POOL_GROUP_DIM, (g + 1) * POOL_GROUP_DIM)
        acc = x[:, cols]
        for k in range(1, w):
            acc = acc + e_ref[HIST_ROWS - k:HIST_ROWS - k + t_rows, cols]
        cnt = jnp.minimum(w, pos + 1).astype(F32)
        mg = (acc / cnt - x[:, cols]).astype(BF16)
        y = _dot(mg, w_ref[g]) * sc_ref[:, cols]
        o_ref[:, cols] = y.astype(o_ref.dtype)
    e_ref[0:HIST_ROWS, :] = e_ref[t_rows:t_rows + HIST_ROWS, :]


def pool_mix(rest, hist, hl, pool_w, pool_scale, l, *, row0, n_seq, seq_len, t_rows, pos0, m_total, prev=None):
    n_t = seq_len // t_rows
    b0 = row0 // t_rows
    row_map = lambda s, t: (b0 + s * n_t + t, 0)
    in_specs = [pl.BlockSpec((t_rows, D_POOL), row_map),
                pl.BlockSpec((None, 1, HIST_ROWS, D_POOL), lambda s, t: (hl, s, 0, 0)),
                pl.BlockSpec((None,) + pool_w.shape[1:], lambda s, t: (l, 0, 0, 0)),
                pl.BlockSpec((1, D_POOL), lambda s, t: (0, 0))]
    args = [rest, hist, pool_w, pool_scale.reshape(1, -1)]
    aliases = {}
    if prev is not None:
        in_specs.append(pl.BlockSpec(memory_space=pl.ANY))
        args.append(prev)
        aliases = {4: 0}
    return pl.pallas_call(
        functools.partial(_pool_kernel, t_rows=t_rows, pos0=pos0, aliased=prev is not None),
        grid=(n_seq, n_t),
        in_specs=in_specs,
        out_specs=pl.BlockSpec((t_rows, D_POOL), row_map),
        out_shape=jax.ShapeDtypeStruct((m_total, D_POOL), BF16),
        scratch_shapes=[pltpu.VMEM((HIST_ROWS + t_rows, D_POOL), F32)],
        input_output_aliases=aliases,
        compiler_params=_params(("parallel", "arbitrary")), name="pool_mix",
    )(*args)


KV_HEADS_PER_TILE = 4


def _kv_up_k_kernel(c_ref, w_ref, kr_ref, o_ref):
    r = _dot(c_ref[...], w_ref[...])
    kr = kr_ref[...].astype(BF16)
    for hh in range(KV_HEADS_PER_TILE):
        o_ref[hh, :, 0:MLA_NOPE_DIM] = r[:, hh * MLA_NOPE_DIM:(hh + 1) * MLA_NOPE_DIM].astype(BF16)
        o_ref[hh, :, MLA_NOPE_DIM:Q_HEAD_PAD] = kr


def _kv_up_v_kernel(c_ref, w_ref, o_ref):
    r = _dot(c_ref[...], w_ref[...])
    for hh in range(KV_HEADS_PER_TILE):
        o_ref[hh] = r[:, hh * MLA_V_DIM:(hh + 1) * MLA_V_DIM].astype(BF16)


def kv_up(ckv_b, kr, wkv2, l, rows, tm=1024):
    kd = ckv_b.shape[1]
    tn = KV_HEADS_PER_TILE * MLA_NOPE_DIM
    nj = MLA_HEADS // KV_HEADS_PER_TILE
    k = pl.pallas_call(
        _kv_up_k_kernel,
        grid=(rows // tm, nj),
        in_specs=[pl.BlockSpec((tm, kd), lambda i, j: (i, 0)),
                  pl.BlockSpec((None, kd, tn), lambda i, j: (l, 0, j)),
                  pl.BlockSpec((tm, LANES), lambda i, j: (i, 0))],
        out_specs=pl.BlockSpec((KV_HEADS_PER_TILE, tm, Q_HEAD_PAD), lambda i, j: (j, i, 0)),
        out_shape=jax.ShapeDtypeStruct((MLA_HEADS, rows, Q_HEAD_PAD), BF16),
        compiler_params=_params(("parallel", "arbitrary")), name="kv_up_k",
    )(ckv_b, wkv2, kr)
    v = pl.pallas_call(
        _kv_up_v_kernel,
        grid=(rows // tm, nj),
        in_specs=[pl.BlockSpec((tm, kd), lambda i, j: (i, 0)),
                  pl.BlockSpec((None, kd, tn), lambda i, j: (l, 0, j + nj))],
        out_specs=pl.BlockSpec((KV_HEADS_PER_TILE, tm, MLA_V_DIM), lambda i, j: (j, i, 0)),
        out_shape=jax.ShapeDtypeStruct((MLA_HEADS, rows, MLA_V_DIM), BF16),
        compiler_params=_params(("parallel", "arbitrary")), name="kv_up_v",
    )(ckv_b, wkv2)
    return k, v


def _attn_prompt_kernel(q_ref, k_ref, v_ref, o_ref, qh_ref, m_ref, l_ref, acc_ref, *, t):
    qi = pl.program_id(1)
    ki = pl.program_id(2)

    @pl.when(ki == 0)
    def _():
        for h in range(MLA_HEADS):
            qh_ref[h] = q_ref[:, h * Q_HEAD_PAD:(h + 1) * Q_HEAD_PAD]
        m_ref[...] = jnp.full_like(m_ref, -jnp.inf)
        l_ref[...] = jnp.zeros_like(l_ref)
        acc_ref[...] = jnp.zeros_like(acc_ref)

    def all_heads(bias):
        def head(h, carry):
            s = _dot_nt(qh_ref[h], k_ref[h]) * MLA_SCALE
            if bias is not None:
                s = s + bias
            m_prev = m_ref[h]
            m_new = jnp.maximum(m_prev, jnp.max(s, axis=-1, keepdims=True))
            alpha = jnp.exp(m_prev - m_new)
            p = jnp.exp(s - jnp.tile(m_new, (1, t // LANES)))
            l_ref[h] = alpha * l_ref[h] + jnp.sum(p, axis=-1, keepdims=True)
            acc_ref[h] = alpha * acc_ref[h] + _dot(p.astype(BF16), v_ref[h])
            m_ref[h] = m_new
            return carry
        lax.fori_loop(0, MLA_HEADS, head, 0, unroll=2)

    @pl.when(ki < qi)
    def _():
        all_heads(None)

    @pl.when(ki == qi)
    def _():
        rc = lax.broadcasted_iota(jnp.int32, (t, t), 0) // CHUNK
        cc = lax.broadcasted_iota(jnp.int32, (t, t), 1) // CHUNK
        all_heads(jnp.where(cc <= rc, 0.0, -jnp.inf))

    @pl.when(ki == pl.num_programs(2) - 1)
    def _():
        for h in range(MLA_HEADS):
            o_ref[:, h * MLA_V_DIM:(h + 1) * MLA_V_DIM] = (acc_ref[h] / l_ref[h]).astype(o_ref.dtype)


def attention_prompt(q, k, v, *, n_batch, seq_len, m_total, t=512):
    nt = seq_len // t
    kmap = lambda b, qi, ki: (0, b * nt + jnp.minimum(ki, qi), 0)
    return pl.pallas_call(
        functools.partial(_attn_prompt_kernel, t=t),
        grid=(n_batch, nt, nt),
        in_specs=[pl.BlockSpec((t, MLA_HEADS * Q_HEAD_PAD), lambda b, qi, ki: (b * nt + qi, 0)),
                  pl.BlockSpec((MLA_HEADS, t, Q_HEAD_PAD), kmap),
                  pl.BlockSpec((MLA_HEADS, t, MLA_V_DIM), kmap)],
        out_specs=pl.BlockSpec((t, D_MLA), lambda b, qi, ki: (b * nt + qi, 0)),
        out_shape=jax.ShapeDtypeStruct((m_total, D_MLA), BF16),
        scratch_shapes=[pltpu.VMEM((MLA_HEADS, t, Q_HEAD_PAD), BF16),
                        pltpu.VMEM((MLA_HEADS, t, LANES), F32),
                        pltpu.VMEM((MLA_HEADS, t, LANES), F32),
                        pltpu.VMEM((MLA_HEADS, t, MLA_V_DIM), F32)],
        compiler_params=_params(("parallel", "parallel", "arbitrary")), name="attn_prompt",
    )(q, k, v)


def _attn_sample_kernel(q_ref, cp_ref, kp_ref, cn_ref, kn_ref, wk_ref, wv_ref, _, o_ref,
                        qa_ref, qr_ref, ol_ref, *, ls):
    q = q_ref[...]
    for h in range(MLA_HEADS):
        rows = slice(h * ls, (h + 1) * ls)
        lo = h * Q_HEAD_PAD
        wk_h = wk_ref[:, h * MLA_NOPE_DIM:(h + 1) * MLA_NOPE_DIM]
        qa_ref[rows, :] = _dot_nt(q[:, lo:lo + MLA_NOPE_DIM], wk_h).astype(BF16)
        qr_ref[rows, :] = q[:, lo + MLA_NOPE_DIM:lo + Q_HEAD_PAD]
    cp = cp_ref[...].astype(BF16)
    kp = kp_ref[...].astype(BF16)
    cn = cn_ref[...]
    kn = kn_ref[...].astype(BF16)
    qa = qa_ref[...]
    qr = qr_ref[...]
    s_p = (_dot_nt(qa, cp) + _dot_nt(qr[:, :MLA_ROPE_DIM], kp)) * MLA_SCALE
    s_n = (_dot_nt(qa, cn) + _dot_nt(qr, kn)) * MLA_SCALE
    m = jnp.maximum(jnp.max(s_p, axis=-1, keepdims=True), jnp.max(s_n, axis=-1, keepdims=True))
    p_p = jnp.exp(s_p - m)
    p_n = jnp.exp(s_n - m)
    den = jnp.sum(p_p, axis=-1, keepdims=True) + jnp.sum(p_n, axis=-1, keepdims=True)
    ol = (_dot(p_p.astype(BF16), cp) + _dot(p_n.astype(BF16), cn)) / den
    ol_ref[...] = ol.astype(BF16)
    for h in range(MLA_HEADS):
        wv_h = wv_ref[:, h * MLA_V_DIM:(h + 1) * MLA_V_DIM]
        o_ref[:, h * MLA_V_DIM:(h + 1) * MLA_V_DIM] = _dot(ol_ref[h * ls:(h + 1) * ls, :], wv_h).astype(o_ref.dtype)


def attention_sample(q, cache_ckv, cache_kr, ckv_b, kr, wkv2, l, prev, *, row0, n_batch, ls):
    past = cache_ckv.shape[2]
    rb0 = row0 // ls
    rmap = lambda b: (rb0 + b, 0)
    hk = MLA_HEADS * MLA_NOPE_DIM
    return pl.pallas_call(
        functools.partial(_attn_sample_kernel, ls=ls),
        grid=(n_batch,),
        in_specs=[pl.BlockSpec((ls, MLA_HEADS * Q_HEAD_PAD), rmap),
                  pl.BlockSpec((None, None, past, MLA_KV_RANK), lambda b: (l, b, 0, 0)),
                  pl.BlockSpec((None, None, past, MLA_ROPE_DIM), lambda b: (l, b, 0, 0)),
                  pl.BlockSpec((ls, MLA_KV_RANK), rmap),
                  pl.BlockSpec((ls, LANES), rmap),
                  pl.BlockSpec((None, MLA_KV_RANK, hk), lambda b: (l, 0, 0)),
                  pl.BlockSpec((None, MLA_KV_RANK, hk), lambda b: (l, 0, 1)),
                  pl.BlockSpec(memory_space=pl.ANY)],
        out_specs=pl.BlockSpec((ls, D_MLA), rmap),
        out_shape=jax.ShapeDtypeStruct(prev.shape, prev.dtype),
        scratch_shapes=[pltpu.VMEM((MLA_HEADS * ls, MLA_KV_RANK), BF16),
                        pltpu.VMEM((MLA_HEADS * ls, LANES), BF16),
                        pltpu.VMEM((MLA_HEADS * ls, MLA_KV_RANK), BF16)],
        input_output_aliases={7: 0},
        compiler_params=_params(("parallel",)), name="attn_sample",
    )(q, cache_ckv, cache_kr, ckv_b, kr, wkv2, wkv2, prev)


def _rope_tables(pos):
    half = MLA_ROPE_DIM // 2
    inv = ROPE_THETA ** (-jnp.arange(half, dtype=F32) / half)
    ang = pos.astype(F32)[:, None] * inv[None, :]
    cos, sin = jnp.cos(ang), jnp.sin(ang)
    z = jnp.zeros((pos.shape[0], LANES - MLA_ROPE_DIM), F32)
    return jnp.concatenate([cos, cos, z], axis=1), jnp.concatenate([-sin, sin, z], axis=1)


def _swap_halves(w):
    half = w.shape[-1] // 2
    return jnp.concatenate([w[..., half:], w[..., :half]], axis=-1)


def kernel(x_prompt, x_sample, cache_mla_ckv, cache_mla_krope, state_pool, g_pre_mix, g_post_mix, g_pre_ffn, g_post_ffn, w_in, cm_ln_g, cm_ln_b, cm_ws, cm_bs, pool_w, pool_scale, mla_q_norm, mla_w_uq, mla_kv_norm, mla_w_ukv, w_out, ffn_w_gate, ffn_w_up, ffn_w_down):
    bp, lp, d = x_prompt.shape
    bs_, ls, _ = x_sample.shape
    past = cache_mla_ckv.shape[2]
    mp, ms = bp * lp, bs_ * ls
    m = mp + ms
    d_ff = ffn_w_gate.shape[2]

    x = jnp.concatenate([x_prompt.reshape(mp, d), x_sample.reshape(ms, d)], axis=0)
    pos = jnp.concatenate([jnp.tile(jnp.arange(lp), bp), jnp.tile(past + jnp.arange(ls), bs_)])
    cos_t, sin_t = _rope_tables(pos)

    w_in_b = w_in.astype(BF16)
    w_kr = w_in[:, :, OFF_KR:OFF_KR + MLA_ROPE_DIM]
    w_kr2 = jnp.concatenate([w_kr, _swap_halves(w_kr)], axis=-1).astype(BF16)
    wq = mla_w_uq.reshape(DEPTH, MLA_Q_RANK, MLA_HEADS, MLA_NOPE_DIM + MLA_ROPE_DIM)
    wq_rope = wq[..., MLA_NOPE_DIM:]
    wq2 = jnp.concatenate([wq[..., :MLA_NOPE_DIM], wq_rope, _swap_halves(wq_rope)], axis=-1)
    wq2 = wq2.reshape(DEPTH, MLA_Q_RANK, MLA_HEADS * Q_HEAD_PAD).astype(BF16)
    wkv = mla_w_ukv.reshape(DEPTH, MLA_KV_RANK, MLA_HEADS, MLA_NOPE_DIM + MLA_V_DIM)
    wkv2 = jnp.concatenate([wkv[..., :MLA_NOPE_DIM].reshape(DEPTH, MLA_KV_RANK, -1),
                            wkv[..., MLA_NOPE_DIM:].reshape(DEPTH, MLA_KV_RANK, -1)], axis=-1).astype(BF16)
    w_out_b = w_out.astype(BF16)
    n_ft = d_ff // FF_TILE
    w_gu = jnp.stack([ffn_w_gate.reshape(DEPTH, d, n_ft, FF_TILE), ffn_w_up.reshape(DEPTH, d, n_ft, FF_TILE)],
                     axis=3).astype(BF16).reshape(DEPTH, d, 2 * d_ff)
    wd_b = ffn_w_down.astype(BF16)
    pool_w_b = pool_w.astype(BF16)
    ws_s = cm_ws.at[:, :, CHUNK:, CHUNK:].set(cm_ws[:, :, :CHUNK, :CHUNK])
    ws2 = jnp.stack([cm_ws, ws_s], axis=1)
    bs_s = jnp.concatenate([cm_bs[:, :, :CHUNK], cm_bs[:, :, :CHUNK]], axis=-1)
    bs2 = jnp.swapaxes(jnp.stack([cm_bs, bs_s], axis=1), -1, -2)
    hist_p = jnp.zeros((1, bp, HIST_ROWS, D_POOL), F32)
    hist_s = jnp.pad(state_pool, ((0, 0), (0, 0), (HIST_ROWS - POOL_HIST, 0), (0, 0)))

    outs = {k: [] for k in ("ckv_p", "kr_p", "pool_p", "ckv_s", "kr_s", "pool_s", "cmv_s")}
    h = prenorm(x, g_pre_mix[0])
    for l in range(DEPTH):
        uv = matmul(h, w_in_b, l, n_off=0, n=OFF_POOL, act="gelu", name="in_uv")
        rest = matmul(h, w_in_b, l, n_off=OFF_POOL, n=OFF_KR - OFF_POOL, name="in_rest")
        kr = matmul_rope(h, w_kr2, l, cos_t, sin_t, head_w=LANES, out_dtype=F32, tn=LANES, name="in_kr")
        a, vn = chunk_mlp(uv, cm_ln_g[l], cm_ln_b[l], ws2, bs2, l, mp // CM_CHUNK)
        b = pool_mix(rest, hist_p, 0, pool_w_b, pool_scale[l], l, row0=0, n_seq=bp, seq_len=lp,
                     t_rows=256, pos0=0, m_total=m)
        b = pool_mix(rest, hist_s, l, pool_w_b, pool_scale[l], l, row0=mp, n_seq=bs_, seq_len=ls,
                     t_rows=ls, pos0=past, m_total=m, prev=b)
        qn, ckv, ckv_b = latnorm(rest, mla_q_norm[l], mla_kv_norm[l])
        q = matmul_rope(qn, wq2, l, cos_t, sin_t, head_w=Q_HEAD_PAD, out_dtype=BF16, name="q_up")
        k_p, v_p = kv_up(ckv_b, kr, wkv2, l, mp)
        c = attention_prompt(q, k_p, v_p, n_batch=bp, seq_len=lp, m_total=m)
        c = attention_sample(q, cache_mla_ckv, cache_mla_krope, ckv_b, kr, wkv2, l, c,
                             row0=mp, n_batch=bs_, ls=ls)
        mix = mix_out(a, b, c, w_out_b, l)
        x, h2 = resnorm(x, mix, g_post_mix[l], g_pre_ffn[l])
        act = matmul(h2, w_gu, l, act="swiglu", out_dtype=BF16, tm=1536, tn=2 * FF_TILE, name="ffn_up")
        f = matmul(act, wd_b, l, tm=512, tn=512, name="ffn_down")
        x, h = resnorm(x, f, g_post_ffn[l], g_pre_mix[l + 1] if l + 1 < DEPTH else None)

        pool_raw = rest[:, :D_POOL]
        outs["ckv_p"].append(ckv[:mp].reshape(bp, lp, MLA_KV_RANK))
        outs["ckv_s"].append(ckv[mp:].reshape(bs_, ls, MLA_KV_RANK))
        outs["kr_p"].append(kr[:mp, :MLA_ROPE_DIM].reshape(bp, lp, MLA_ROPE_DIM))
        outs["kr_s"].append(kr[mp:, :MLA_ROPE_DIM].reshape(bs_, ls, MLA_ROPE_DIM))
        outs["pool_p"].append(pool_raw[:mp].reshape(bp, lp, D_POOL)[:, lp - POOL_HIST:])
        outs["pool_s"].append(pool_raw[mp:].reshape(bs_, ls, D_POOL)[:, ls - POOL_HIST:])
        outs["cmv_s"].append(vn[mp:].reshape(bs_, ls, D_CM))

    return (x[:mp].reshape(bp, lp, d), x[mp:].reshape(bs_, ls, d),
            jnp.stack(outs["ckv_p"]), jnp.stack(outs["kr_p"]), jnp.stack(outs["pool_p"]),
            jnp.stack(outs["ckv_s"]), jnp.stack(outs["kr_s"]), jnp.stack(outs["pool_s"]),
            jnp.stack(outs["cmv_s"]))
```

```python
import functools

import jax
import jax.numpy as jnp
from jax import lax
from jax.experimental import pallas as pl
from jax.experimental.pallas import tpu as pltpu

D_MODEL = 4096
DEPTH = 4
CHUNK = 64
EPS = 1e-6
CM_CHUNK = 128
CM_HEADS = 8
CM_HEAD_DIM = 128
D_CM = CM_HEADS * CM_HEAD_DIM
POOL_WINDOWS = (2, 4, 8, 16)
POOL_GROUP_DIM = 256
D_POOL = len(POOL_WINDOWS) * POOL_GROUP_DIM
POOL_HIST = max(POOL_WINDOWS) - 1
HIST_ROWS = POOL_HIST + 1
MLA_HEADS = 16
MLA_Q_RANK = 1024
MLA_KV_RANK = 512
MLA_NOPE_DIM = 128
MLA_ROPE_DIM = 64
MLA_V_DIM = 128
D_MLA = MLA_HEADS * MLA_V_DIM
MLA_SCALE = (MLA_NOPE_DIM + MLA_ROPE_DIM) ** -0.5
Q_SCALE = MLA_SCALE * 1.4426950408889634
ROPE_THETA = 10000.0
Q_HEAD_PAD = 256
OFF_V = D_CM
OFF_POOL = 2 * D_CM
OFF_Q = OFF_POOL + D_POOL
OFF_CKV = OFF_Q + MLA_Q_RANK
OFF_KR = OFF_CKV + MLA_KV_RANK
FF_TILE = 256

LANES = 128
VMEM_LIMIT = 56 * 1024 * 1024

BF16 = jnp.bfloat16
F32 = jnp.float32


def _params(sem):
    return pltpu.CompilerParams(dimension_semantics=sem, vmem_limit_bytes=VMEM_LIMIT)


def _rms(x, g):
    r = lax.rsqrt(jnp.mean(x * x, axis=-1, keepdims=True) + EPS)
    return (x * r) * g


def _dot(a, b):
    return jnp.dot(a, b, preferred_element_type=F32)


def _dot_nt(a, b):
    return lax.dot_general(a, b, (((1,), (1,)), ((), ())), preferred_element_type=F32)


def _rope_lanes(b, c, s):
    return b * c + pltpu.roll(b, 64, axis=1) * s


def _two_source_specs(tm, d, n_first):
    return [pl.BlockSpec((tm, d), lambda i: (jnp.minimum(i, n_first - 1), 0)),
            pl.BlockSpec((tm, d), lambda i: (jnp.maximum(i - n_first, 0), 0))]


def _prenorm_kernel(xp_ref, xs_ref, g_ref, h_ref, *, n_first):
    x = jnp.where(pl.program_id(0) < n_first, xp_ref[...], xs_ref[...])
    h_ref[...] = _rms(x, g_ref[...]).astype(BF16)


def prenorm(xp, xs, g, tm=512):
    d = xp.shape[1]
    m = xp.shape[0] + xs.shape[0]
    n_first = xp.shape[0] // tm
    return pl.pallas_call(
        functools.partial(_prenorm_kernel, n_first=n_first),
        grid=(m // tm,),
        in_specs=_two_source_specs(tm, d, n_first) + [pl.BlockSpec((1, d), lambda i: (0, 0))],
        out_specs=pl.BlockSpec((tm, d), lambda i: (i, 0)),
        out_shape=jax.ShapeDtypeStruct((m, d), BF16),
        compiler_params=_params(("arbitrary",)),
        name="prenorm",
    )(xp, xs, g.reshape(1, d))


def _resnorm_kernel(*refs, n_first, split_in, split_out, with_h):
    refs = list(refs)
    if split_in:
        xp_ref, xs_ref = refs[:2]
        x = jnp.where(pl.program_id(0) < n_first, xp_ref[...], xs_ref[...])
        refs = refs[2:]
    else:
        x = refs[0][...]
        refs = refs[1:]
    f_ref, gpost_ref = refs[:2]
    refs = refs[2:]
    if with_h:
        gpre_ref = refs[0]
        refs = refs[1:]
    xn = x + _rms(f_ref[...], gpost_ref[...])
    if split_out:
        xop_ref, xos_ref = refs[:2]
        refs = refs[2:]

        @pl.when(pl.program_id(0) < n_first)
        def _():
            xop_ref[...] = xn

        @pl.when(pl.program_id(0) >= n_first)
        def _():
            xos_ref[...] = xn
    else:
        refs[0][...] = xn
        refs = refs[1:]
    if with_h:
        refs[0][...] = _rms(xn, gpre_ref[...]).astype(BF16)


def resnorm(x, f, g_post, g_pre, *, split_rows=None, split_out=False, tm=256):
    split_in = isinstance(x, (tuple, list))
    m, d = f.shape
    with_h = g_pre is not None
    n_first = (x[0].shape[0] if split_in else split_rows if split_out else m) // tm
    row = pl.BlockSpec((tm, d), lambda i: (i, 0))
    vec = pl.BlockSpec((1, d), lambda i: (0, 0))
    in_specs = (_two_source_specs(tm, d, n_first) if split_in else [row]) + [row, vec]
    args = (list(x) if split_in else [x]) + [f, g_post.reshape(1, d)]
    if with_h:
        in_specs.append(vec)
        args.append(g_pre.reshape(1, d))
    if split_out:
        out_specs = _two_source_specs(tm, d, n_first)
        out_shape = [jax.ShapeDtypeStruct((split_rows, d), F32), jax.ShapeDtypeStruct((m - split_rows, d), F32)]
    else:
        out_specs = [row]
        out_shape = [jax.ShapeDtypeStruct((m, d), F32)]
    if with_h:
        out_specs.append(row)
        out_shape.append(jax.ShapeDtypeStruct((m, d), BF16))
    outs = pl.pallas_call(
        functools.partial(_resnorm_kernel, n_first=n_first, split_in=split_in, split_out=split_out, with_h=with_h),
        grid=(m // tm,), in_specs=in_specs, out_specs=out_specs, out_shape=out_shape,
        compiler_params=_params(("arbitrary",)), name="resnorm",
    )(*args)
    xo = tuple(outs[:2]) if split_out else outs[0]
    return xo, (outs[-1] if with_h else None)


def _latnorm_kernel(q_ref, c_ref, gq_ref, gc_ref, qn_ref, cn_ref, cb_ref):
    qn_ref[...] = _rms(q_ref[...], gq_ref[...]).astype(BF16)
    cn = _rms(c_ref[...], gc_ref[...])
    cn_ref[...] = cn
    cb_ref[...] = cn.astype(BF16)


def latnorm(rest, gq, gc, tm=512):
    m = rest.shape[0]
    qb = (OFF_Q - OFF_POOL) // MLA_Q_RANK
    cb = (OFF_CKV - OFF_POOL) // MLA_KV_RANK
    return pl.pallas_call(
        _latnorm_kernel,
        grid=(m // tm,),
        in_specs=[pl.BlockSpec((tm, MLA_Q_RANK), lambda i: (i, qb)),
                  pl.BlockSpec((tm, MLA_KV_RANK), lambda i: (i, cb)),
                  pl.BlockSpec((1, MLA_Q_RANK), lambda i: (0, 0)),
                  pl.BlockSpec((1, MLA_KV_RANK), lambda i: (0, 0))],
        out_specs=[pl.BlockSpec((tm, MLA_Q_RANK), lambda i: (i, 0)),
                   pl.BlockSpec((tm, MLA_KV_RANK), lambda i: (i, 0)),
                   pl.BlockSpec((tm, MLA_KV_RANK), lambda i: (i, 0))],
        out_shape=[jax.ShapeDtypeStruct((m, MLA_Q_RANK), BF16),
                   jax.ShapeDtypeStruct((m, MLA_KV_RANK), F32),
                   jax.ShapeDtypeStruct((m, MLA_KV_RANK), BF16)],
        compiler_params=_params(("parallel",)), name="latnorm",
    )(rest, rest, gq.reshape(1, -1), gc.reshape(1, -1))


def _mm_kernel(a_ref, w_ref, o_ref, *, act):
    r = _dot(a_ref[...], w_ref[...].astype(BF16))
    if act == "gelu":
        r = jax.nn.gelu(r)
    o_ref[...] = r.astype(o_ref.dtype)


def matmul(a, w, l, *, n_off=0, n=None, act=None, out_dtype=F32, tm=1024, tn=512, name="mm"):
    m, k = a.shape
    n = w.shape[2] if n is None else n
    jo = n_off // tn
    return pl.pallas_call(
        functools.partial(_mm_kernel, act=act),
        grid=(m // tm, n // tn),
        in_specs=[pl.BlockSpec((tm, k), lambda i, j: (i, 0)),
                  pl.BlockSpec((None, k, tn), lambda i, j: (l, 0, j + jo))],
        out_specs=pl.BlockSpec((tm, tn), lambda i, j: (i, j)),
        out_shape=jax.ShapeDtypeStruct((m, n), out_dtype),
        compiler_params=_params(("parallel", "arbitrary")), name=name,
    )(a, w)


def _ffn_up_kernel(h_ref, wg_ref, wu_ref, o_ref):
    h = h_ref[...]
    g = _dot(h, wg_ref[...].astype(BF16))
    u = _dot(h, wu_ref[...].astype(BF16))
    o_ref[...] = (jax.nn.silu(g) * u).astype(o_ref.dtype)


def ffn_up(h, wg, wu, l, tm=1536, tn=FF_TILE):
    m, k = h.shape
    n = wg.shape[2]
    wspec = pl.BlockSpec((None, k, tn), lambda i, j: (l, 0, j))
    return pl.pallas_call(
        _ffn_up_kernel,
        grid=(m // tm, n // tn),
        in_specs=[pl.BlockSpec((tm, k), lambda i, j: (i, 0)), wspec, wspec],
        out_specs=pl.BlockSpec((tm, tn), lambda i, j: (i, j)),
        out_shape=jax.ShapeDtypeStruct((m, n), BF16),
        compiler_params=_params(("parallel", "arbitrary")), name="ffn_up",
    )(h, wg, wu)


def _mm_rope_kernel(a_ref, w_ref, c_ref, s_ref, o_ref, *, head_w, scale):
    r = _dot(a_ref[...], w_ref[...])
    if scale is not None:
        r = r * scale
    c = c_ref[...]
    s = s_ref[...]
    for h in range(r.shape[1] // head_w):
        lo = h * head_w
        if head_w > LANES:
            o_ref[:, lo:lo + head_w - LANES] = r[:, lo:lo + head_w - LANES].astype(o_ref.dtype)
        b = r[:, lo + head_w - LANES:lo + head_w]
        o_ref[:, lo + head_w - LANES:lo + head_w] = _rope_lanes(b, c, s).astype(o_ref.dtype)


def matmul_rope(a, w, l, cos_t, sin_t, *, head_w, out_dtype, scale=None, tm=1024, tn=512, name="mm_rope"):
    m, k = a.shape
    n = w.shape[2]
    return pl.pallas_call(
        functools.partial(_mm_rope_kernel, head_w=head_w, scale=scale),
        grid=(m // tm, n // tn),
        in_specs=[pl.BlockSpec((tm, k), lambda i, j: (i, 0)),
                  pl.BlockSpec((None, k, tn), lambda i, j: (l, 0, j)),
                  pl.BlockSpec((tm, LANES), lambda i, j: (i, 0)),
                  pl.BlockSpec((tm, LANES), lambda i, j: (i, 0))],
        out_specs=pl.BlockSpec((tm, tn), lambda i, j: (i, j)),
        out_shape=jax.ShapeDtypeStruct((m, n), out_dtype),
        compiler_params=_params(("parallel", "arbitrary")), name=name,
    )(a, w, cos_t, sin_t)


def _mix_out_kernel(a_ref, b_ref, c_ref, wa_ref, wb_ref, wc_ref, o_ref):
    o_ref[...] = (_dot(a_ref[...], wa_ref[...].astype(BF16)) + _dot(b_ref[...], wb_ref[...].astype(BF16))
                  + _dot(c_ref[...], wc_ref[...].astype(BF16)))


def mix_out(a, b, c, w_out, l, tm=1024, tn=512):
    m = a.shape[0]
    n = w_out.shape[2]
    ka, kb, kc = a.shape[1], b.shape[1], c.shape[1]
    return pl.pallas_call(
        _mix_out_kernel,
        grid=(m // tm, n // tn),
        in_specs=[pl.BlockSpec((tm, ka), lambda i, j: (i, 0)),
                  pl.BlockSpec((tm, kb), lambda i, j: (i, 0)),
                  pl.BlockSpec((tm, kc), lambda i, j: (i, 0)),
                  pl.BlockSpec((None, ka, tn), lambda i, j: (l, 0, j)),
                  pl.BlockSpec((None, kb, tn), lambda i, j: (l, ka // kb, j)),
                  pl.BlockSpec((None, kc, tn), lambda i, j: (l, (ka + kb) // kc, j))],
        out_specs=pl.BlockSpec((tm, tn), lambda i, j: (i, j)),
        out_shape=jax.ShapeDtypeStruct((m, n), F32),
        compiler_params=_params(("parallel", "arbitrary")), name="mix_out",
    )(a, b, c, w_out, w_out, w_out)


def _chunk_mlp_kernel(u_ref, v_ref, g_ref, b_ref, w_ref, bs_ref, a_ref, vn_ref, *, first_sample_block):
    is_sample = pl.program_id(0) >= first_sample_block
    v = v_ref[...]
    mu = jnp.mean(v, axis=-1, keepdims=True)
    d = v - mu
    var = jnp.mean(d * d, axis=-1, keepdims=True)
    vn = (d * lax.rsqrt(var + EPS)) * g_ref[...] + b_ref[...]
    vn_ref[...] = vn
    vb = vn.astype(BF16)
    ri = lax.broadcasted_iota(jnp.int32, (CM_CHUNK, CM_CHUNK), 0) // CHUNK
    ci = lax.broadcasted_iota(jnp.int32, (CM_CHUNK, CM_CHUNK), 1) // CHUNK
    max_lag = jnp.where(is_sample, 0, 1)
    keep = (ri - ci >= 0) & (ri - ci <= max_lag)
    bs = bs_ref[0]
    for h in range(CM_HEADS):
        cols = slice(h * CM_HEAD_DIM, (h + 1) * CM_HEAD_DIM)
        w = jnp.where(keep, w_ref[0, h], 0.0).astype(BF16)
        sv = _dot(w, vb[:, cols]) + bs[:, h:h + 1]
        a_ref[:, cols] = (u_ref[:, cols] * sv).astype(a_ref.dtype)


def chunk_mlp(uv, ln_g, ln_b, ws2, bs2, l, first_sample_block):
    m = uv.shape[0]
    sel = lambda c: jnp.where(c >= first_sample_block, 1, 0)
    return pl.pallas_call(
        functools.partial(_chunk_mlp_kernel, first_sample_block=first_sample_block),
        grid=(m // CM_CHUNK,),
        in_specs=[pl.BlockSpec((CM_CHUNK, D_CM), lambda c: (c, 0)),
                  pl.BlockSpec((CM_CHUNK, D_CM), lambda c: (c, 1)),
                  pl.BlockSpec((1, D_CM), lambda c: (0, 0)),
                  pl.BlockSpec((1, D_CM), lambda c: (0, 0)),
                  pl.BlockSpec((None, 1, CM_HEADS, CM_CHUNK, CM_CHUNK), lambda c: (l, sel(c), 0, 0, 0)),
                  pl.BlockSpec((None, 1, CM_CHUNK, CM_HEADS), lambda c: (l, sel(c), 0, 0))],
        out_specs=[pl.BlockSpec((CM_CHUNK, D_CM), lambda c: (c, 0)),
                   pl.BlockSpec((CM_CHUNK, D_CM), lambda c: (c, 0))],
        out_shape=[jax.ShapeDtypeStruct((m, D_CM), BF16), jax.ShapeDtypeStruct((m, D_CM), F32)],
        compiler_params=_params(("parallel",)), name="chunk_mlp",
    )(uv, uv, ln_g.reshape(1, -1), ln_b.reshape(1, -1), ws2, bs2)


def _pool_kernel(*refs, t_rows, pos0, aliased):
    if aliased:
        x_ref, hist_ref, w_ref, sc_ref, _, o_ref, e_ref = refs
    else:
        x_ref, hist_ref, w_ref, sc_ref, o_ref, e_ref = refs
    t = pl.program_id(1)

    @pl.when(t == 0)
    def _():
        e_ref[0:HIST_ROWS, :] = hist_ref[0]

    x = x_ref[...]
    e_ref[HIST_ROWS:HIST_ROWS + t_rows, :] = x
    pos = pos0 + t * t_rows + lax.broadcasted_iota(jnp.int32, (t_rows, 1), 0)
    for g, w in enumerate(POOL_WINDOWS):
        cols = slice(g * POOL_GROUP_DIM, (g + 1) * POOL_GROUP_DIM)
        acc = x[:, cols]
        for k in range(1, w):
            acc = acc + e_ref[HIST_ROWS - k:HIST_ROWS - k + t_rows, cols]
        cnt = jnp.minimum(w, pos + 1).astype(F32)
        mg = (acc / cnt - x[:, cols]).astype(BF16)
        y = _dot(mg, w_ref[g]) * sc_ref[:, cols]
        o_ref[:, cols] = y.astype(o_ref.dtype)
    e_ref[0:HIST_ROWS, :] = e_ref[t_rows:t_rows + HIST_ROWS, :]


def pool_mix(rest, hist, hl, pool_w, pool_scale, l, *, row0, n_seq, seq_len, t_rows, pos0, m_total, prev=None):
    n_t = seq_len // t_rows
    b0 = row0 // t_rows
    row_map = lambda s, t: (b0 + s * n_t + t, 0)
    in_specs = [pl.BlockSpec((t_rows, D_POOL), row_map),
                pl.BlockSpec((None, 1, HIST_ROWS, D_POOL), lambda s, t: (hl, s, 0, 0)),
                pl.BlockSpec((None,) + pool_w.shape[1:], lambda s, t: (l, 0, 0, 0)),
                pl.BlockSpec((1, D_POOL), lambda s, t: (0, 0))]
    args = [rest, hist, pool_w, pool_scale.reshape(1, -1)]
    aliases = {}
    if prev is not None:
        in_specs.append(pl.BlockSpec(memory_space=pl.ANY))
        args.append(prev)
        aliases = {4: 0}
    return pl.pallas_call(
        functools.partial(_pool_kernel, t_rows=t_rows, pos0=pos0, aliased=prev is not None),
        grid=(n_seq, n_t),
        in_specs=in_specs,
        out_specs=pl.BlockSpec((t_rows, D_POOL), row_map),
        out_shape=jax.ShapeDtypeStruct((m_total, D_POOL), BF16),
        scratch_shapes=[pltpu.VMEM((HIST_ROWS + t_rows, D_POOL), F32)],
        input_output_aliases=aliases,
        compiler_params=_params(("parallel", "arbitrary")), name="pool_mix",
    )(*args)


KV_HEADS_PER_TILE = 4


def _kv_up_k_kernel(c_ref, w_ref, kr_ref, o_ref):
    r = _dot(c_ref[...], w_ref[...])
    kr = kr_ref[...].astype(BF16)
    for hh in range(KV_HEADS_PER_TILE):
        o_ref[hh, :, 0:MLA_NOPE_DIM] = r[:, hh * MLA_NOPE_DIM:(hh + 1) * MLA_NOPE_DIM].astype(BF16)
        o_ref[hh, :, MLA_NOPE_DIM:Q_HEAD_PAD] = kr


def _kv_up_v_kernel(c_ref, w_ref, o_ref):
    r = _dot(c_ref[...], w_ref[...])
    for hh in range(KV_HEADS_PER_TILE):
        o_ref[hh] = r[:, hh * MLA_V_DIM:(hh + 1) * MLA_V_DIM].astype(BF16)


def kv_up(ckv_b, kr, wkv2, l, rows, tm=1024):
    kd = ckv_b.shape[1]
    tn = KV_HEADS_PER_TILE * MLA_NOPE_DIM
    nj = MLA_HEADS // KV_HEADS_PER_TILE
    k = pl.pallas_call(
        _kv_up_k_kernel,
        grid=(rows // tm, nj),
        in_specs=[pl.BlockSpec((tm, kd), lambda i, j: (i, 0)),
                  pl.BlockSpec((None, kd, tn), lambda i, j: (l, 0, j)),
                  pl.BlockSpec((tm, LANES), lambda i, j: (i, 0))],
        out_specs=pl.BlockSpec((KV_HEADS_PER_TILE, tm, Q_HEAD_PAD), lambda i, j: (j, i, 0)),
        out_shape=jax.ShapeDtypeStruct((MLA_HEADS, rows, Q_HEAD_PAD), BF16),
        compiler_params=_params(("parallel", "arbitrary")), name="kv_up_k",
    )(ckv_b, wkv2, kr)
    v = pl.pallas_call(
        _kv_up_v_kernel,
        grid=(rows // tm, nj),
        in_specs=[pl.BlockSpec((tm, kd), lambda i, j: (i, 0)),
                  pl.BlockSpec((None, kd, tn), lambda i, j: (l, 0, j + nj))],
        out_specs=pl.BlockSpec((KV_HEADS_PER_TILE, tm, MLA_V_DIM), lambda i, j: (j, i, 0)),
        out_shape=jax.ShapeDtypeStruct((MLA_HEADS, rows, MLA_V_DIM), BF16),
        compiler_params=_params(("parallel", "arbitrary")), name="kv_up_v",
    )(ckv_b, wkv2)
    return k, v


def _attn_prompt_kernel(q_ref, k_ref, v_ref, o_ref, qh_ref, m_ref, l_ref, acc_ref, *, t):
    qi = pl.program_id(1)
    ki = pl.program_id(2)

    @pl.when(ki == 0)
    def _():
        for h in range(MLA_HEADS):
            qh_ref[h] = q_ref[:, h * Q_HEAD_PAD:(h + 1) * Q_HEAD_PAD]
        m_ref[...] = jnp.full_like(m_ref, -jnp.inf)
        l_ref[...] = jnp.zeros_like(l_ref)
        acc_ref[...] = jnp.zeros_like(acc_ref)

    def all_heads(bias):
        def head(h, carry):
            s = _dot_nt(qh_ref[h], k_ref[h])
            if bias is not None:
                s = s + bias
            m_prev = m_ref[h]
            m_new = jnp.maximum(m_prev, jnp.max(s, axis=-1, keepdims=True))
            alpha = jnp.exp2(m_prev - m_new)
            p = jnp.exp2(s - jnp.tile(m_new, (1, t // LANES)))
            l_ref[h] = alpha * l_ref[h] + jnp.sum(p, axis=-1, keepdims=True)
            acc_ref[h] = alpha * acc_ref[h] + _dot(p.astype(BF16), v_ref[h])
            m_ref[h] = m_new
            return carry
        lax.fori_loop(0, MLA_HEADS, head, 0, unroll=2)

    @pl.when(ki < qi)
    def _():
        all_heads(None)

    @pl.when(ki == qi)
    def _():
        rc = lax.broadcasted_iota(jnp.int32, (t, t), 0) // CHUNK
        cc = lax.broadcasted_iota(jnp.int32, (t, t), 1) // CHUNK
        all_heads(jnp.where(cc <= rc, 0.0, -jnp.inf))

    @pl.when(ki == pl.num_programs(2) - 1)
    def _():
        for h in range(MLA_HEADS):
            o_ref[:, h * MLA_V_DIM:(h + 1) * MLA_V_DIM] = (acc_ref[h] / l_ref[h]).astype(o_ref.dtype)


def attention_prompt(q, k, v, *, n_batch, seq_len, m_total, t=512):
    nt = seq_len // t
    kmap = lambda b, qi, ki: (0, b * nt + jnp.minimum(ki, qi), 0)
    return pl.pallas_call(
        functools.partial(_attn_prompt_kernel, t=t),
        grid=(n_batch, nt, nt),
        in_specs=[pl.BlockSpec((t, MLA_HEADS * Q_HEAD_PAD), lambda b, qi, ki: (b * nt + qi, 0)),
                  pl.BlockSpec((MLA_HEADS, t, Q_HEAD_PAD), kmap),
                  pl.BlockSpec((MLA_HEADS, t, MLA_V_DIM), kmap)],
        out_specs=pl.BlockSpec((t, D_MLA), lambda b, qi, ki: (b * nt + qi, 0)),
        out_shape=jax.ShapeDtypeStruct((m_total, D_MLA), BF16),
        scratch_shapes=[pltpu.VMEM((MLA_HEADS, t, Q_HEAD_PAD), BF16),
                        pltpu.VMEM((MLA_HEADS, t, LANES), F32),
                        pltpu.VMEM((MLA_HEADS, t, LANES), F32),
                        pltpu.VMEM((MLA_HEADS, t, MLA_V_DIM), F32)],
        compiler_params=_params(("parallel", "parallel", "arbitrary")), name="attn_prompt",
    )(q, k, v)


def _attn_sample_kernel(q_ref, cp_ref, kp_ref, cn_ref, kn_ref, wk_ref, wv_ref, _, o_ref,
                        qa_ref, qr_ref, ol_ref, *, ls):
    q = q_ref[...]
    for h in range(MLA_HEADS):
        rows = slice(h * ls, (h + 1) * ls)
        lo = h * Q_HEAD_PAD
        wk_h = wk_ref[:, h * MLA_NOPE_DIM:(h + 1) * MLA_NOPE_DIM]
        qa_ref[rows, :] = _dot_nt(q[:, lo:lo + MLA_NOPE_DIM], wk_h).astype(BF16)
        qr_ref[rows, :] = q[:, lo + MLA_NOPE_DIM:lo + Q_HEAD_PAD]
    cp = cp_ref[...].astype(BF16)
    kp = kp_ref[...].astype(BF16)
    cn = cn_ref[...]
    kn = kn_ref[...].astype(BF16)
    qa = qa_ref[...]
    qr = qr_ref[...]
    s_p = _dot_nt(qa, cp) + _dot_nt(qr[:, :MLA_ROPE_DIM], kp)
    s_n = _dot_nt(qa, cn) + _dot_nt(qr, kn)
    m = jnp.maximum(jnp.max(s_p, axis=-1, keepdims=True), jnp.max(s_n, axis=-1, keepdims=True))
    p_p = jnp.exp2(s_p - m)
    p_n = jnp.exp2(s_n - m)
    den = jnp.sum(p_p, axis=-1, keepdims=True) + jnp.sum(p_n, axis=-1, keepdims=True)
    ol = (_dot(p_p.astype(BF16), cp) + _dot(p_n.astype(BF16), cn)) / den
    ol_ref[...] = ol.astype(BF16)
    for h in range(MLA_HEADS):
        wv_h = wv_ref[:, h * MLA_V_DIM:(h + 1) * MLA_V_DIM]
        o_ref[:, h * MLA_V_DIM:(h + 1) * MLA_V_DIM] = _dot(ol_ref[h * ls:(h + 1) * ls, :], wv_h).astype(o_ref.dtype)


def attention_sample(q, cache_ckv, cache_kr, ckv_b, kr, wkv2, l, prev, *, row0, n_batch, ls):
    past = cache_ckv.shape[2]
    rb0 = row0 // ls
    rmap = lambda b: (rb0 + b, 0)
    hk = MLA_HEADS * MLA_NOPE_DIM
    return pl.pallas_call(
        functools.partial(_attn_sample_kernel, ls=ls),
        grid=(n_batch,),
        in_specs=[pl.BlockSpec((ls, MLA_HEADS * Q_HEAD_PAD), rmap),
                  pl.BlockSpec((None, None, past, MLA_KV_RANK), lambda b: (l, b, 0, 0)),
                  pl.BlockSpec((None, None, past, MLA_ROPE_DIM), lambda b: (l, b, 0, 0)),
                  pl.BlockSpec((ls, MLA_KV_RANK), rmap),
                  pl.BlockSpec((ls, LANES), rmap),
                  pl.BlockSpec((None, MLA_KV_RANK, hk), lambda b: (l, 0, 0)),
                  pl.BlockSpec((None, MLA_KV_RANK, hk), lambda b: (l, 0, 1)),
                  pl.BlockSpec(memory_space=pl.ANY)],
        out_specs=pl.BlockSpec((ls, D_MLA), rmap),
        out_shape=jax.ShapeDtypeStruct(prev.shape, prev.dtype),
        scratch_shapes=[pltpu.VMEM((MLA_HEADS * ls, MLA_KV_RANK), BF16),
                        pltpu.VMEM((MLA_HEADS * ls, LANES), BF16),
                        pltpu.VMEM((MLA_HEADS * ls, MLA_KV_RANK), BF16)],
        input_output_aliases={7: 0},
        compiler_params=_params(("parallel",)), name="attn_sample",
    )(q, cache_ckv, cache_kr, ckv_b, kr, wkv2, wkv2, prev)


def _rope_tables(pos):
    half = MLA_ROPE_DIM // 2
    inv = ROPE_THETA ** (-jnp.arange(half, dtype=F32) / half)
    ang = pos.astype(F32)[:, None] * inv[None, :]
    cos, sin = jnp.cos(ang), jnp.sin(ang)
    z = jnp.zeros((pos.shape[0], LANES - MLA_ROPE_DIM), F32)
    return jnp.concatenate([cos, cos, z], axis=1), jnp.concatenate([-sin, sin, z], axis=1)


def _swap_halves(w):
    half = w.shape[-1] // 2
    return jnp.concatenate([w[..., half:], w[..., :half]], axis=-1)


def kernel(x_prompt, x_sample, cache_mla_ckv, cache_mla_krope, state_pool, g_pre_mix, g_post_mix, g_pre_ffn, g_post_ffn, w_in, cm_ln_g, cm_ln_b, cm_ws, cm_bs, pool_w, pool_scale, mla_q_norm, mla_w_uq, mla_kv_norm, mla_w_ukv, w_out, ffn_w_gate, ffn_w_up, ffn_w_down):
    bp, lp, d = x_prompt.shape
    bs_, ls, _ = x_sample.shape
    past = cache_mla_ckv.shape[2]
    mp, ms = bp * lp, bs_ * ls
    m = mp + ms

    x = (x_prompt.reshape(mp, d), x_sample.reshape(ms, d))
    pos = jnp.concatenate([jnp.tile(jnp.arange(lp), bp), jnp.tile(past + jnp.arange(ls), bs_)])
    cos_t, sin_t = _rope_tables(pos)

    w_kr =w_in[:, :, OFF_KR:OFF_KR + MLA_ROPE_DIM]
    w_kr2 = jnp.concatenate([w_kr, _swap_halves(w_kr)], axis=-1).astype(BF16)
    wq = mla_w_uq.reshape(DEPTH, MLA_Q_RANK, MLA_HEADS, MLA_NOPE_DIM + MLA_ROPE_DIM)
    wq_rope = wq[..., MLA_NOPE_DIM:]
    wq2 = jnp.concatenate([wq[..., :MLA_NOPE_DIM], wq_rope, _swap_halves(wq_rope)], axis=-1)
    wq2 = wq2.reshape(DEPTH, MLA_Q_RANK, MLA_HEADS * Q_HEAD_PAD).astype(BF16)
    wkv = mla_w_ukv.reshape(DEPTH, MLA_KV_RANK, MLA_HEADS, MLA_NOPE_DIM + MLA_V_DIM)
    wkv2 = jnp.concatenate([wkv[..., :MLA_NOPE_DIM].reshape(DEPTH, MLA_KV_RANK, -1),
                            wkv[..., MLA_NOPE_DIM:].reshape(DEPTH, MLA_KV_RANK, -1)], axis=-1).astype(BF16)
    wd_b = ffn_w_down.astype(BF16)
    pool_w_b = pool_w.astype(BF16)
    ws_s = cm_ws.at[:, :, CHUNK:, CHUNK:].set(cm_ws[:, :, :CHUNK, :CHUNK])
    ws2 = jnp.stack([cm_ws, ws_s], axis=1)
    bs_s = jnp.concatenate([cm_bs[:, :, :CHUNK], cm_bs[:, :, :CHUNK]], axis=-1)
    bs2 = jnp.swapaxes(jnp.stack([cm_bs, bs_s], axis=1), -1, -2)
    hist_p = jnp.zeros((1, bp, HIST_ROWS, D_POOL), F32)
    hist_s = jnp.pad(state_pool, ((0, 0), (0, 0), (HIST_ROWS - POOL_HIST, 0), (0, 0)))

    outs = {k: [] for k in ("ckv_p", "kr_p", "pool_p", "ckv_s", "kr_s", "pool_s", "cmv_s")}
    h = prenorm(x[0], x[1], g_pre_mix[0])
    for l in range(DEPTH):
        last = l + 1 == DEPTH
        uv = matmul(h, w_in, l, n_off=0, n=OFF_POOL, act="gelu", name="in_uv")
        rest = matmul(h, w_in, l, n_off=OFF_POOL, n=OFF_KR - OFF_POOL, name="in_rest")
        kr = matmul_rope(h, w_kr2, l, cos_t, sin_t, head_w=LANES, out_dtype=F32, tn=LANES, name="in_kr")
        a, vn = chunk_mlp(uv, cm_ln_g[l], cm_ln_b[l], ws2, bs2, l, mp // CM_CHUNK)
        b = pool_mix(rest, hist_p, 0, pool_w_b, pool_scale[l], l, row0=0, n_seq=bp, seq_len=lp,
                     t_rows=256, pos0=0, m_total=m)
        b = pool_mix(rest, hist_s, l, pool_w_b, pool_scale[l], l, row0=mp, n_seq=bs_, seq_len=ls,
                     t_rows=ls, pos0=past, m_total=m, prev=b)
        qn, ckv, ckv_b = latnorm(rest, mla_q_norm[l], mla_kv_norm[l])
        q = matmul_rope(qn, wq2, l, cos_t, sin_t, head_w=Q_HEAD_PAD, out_dtype=BF16, scale=Q_SCALE, name="q_up")
        k_p, v_p = kv_up(ckv_b, kr, wkv2, l, mp)
        c = attention_prompt(q, k_p, v_p, n_batch=bp, seq_len=lp, m_total=m)
        c = attention_sample(q, cache_mla_ckv, cache_mla_krope, ckv_b, kr, wkv2, l, c,
                             row0=mp, n_batch=bs_, ls=ls)
        mix = mix_out(a, b, c, w_out, l)
        x, h2 = resnorm(x, mix, g_post_mix[l], g_pre_ffn[l])
        act = ffn_up(h2, ffn_w_gate, ffn_w_up, l)
        f = matmul(act, wd_b, l, tm=512, tn=512, name="ffn_down")
        x, h = resnorm(x, f, g_post_ffn[l], None if last else g_pre_mix[l + 1],
                       split_rows=mp, split_out=last)

        pool_raw = rest[:, :D_POOL]
        outs["ckv_p"].append(ckv[:mp].reshape(bp, lp, MLA_KV_RANK))
        outs["ckv_s"].append(ckv[mp:].reshape(bs_, ls, MLA_KV_RANK))
        outs["kr_p"].append(kr[:mp, :MLA_ROPE_DIM].reshape(bp, lp, MLA_ROPE_DIM))
        outs["kr_s"].append(kr[mp:, :MLA_ROPE_DIM].reshape(bs_, ls, MLA_ROPE_DIM))
        outs["pool_p"].append(pool_raw[:mp].reshape(bp, lp, D_POOL)[:, lp - POOL_HIST:])
        outs["pool_s"].append(pool_raw[mp:].reshape(bs_, ls, D_POOL)[:, ls - POOL_HIST:])
        outs["cmv_s"].append(vn[mp:].reshape(bs_, ls, D_CM))

    return (x[0].reshape(bp, lp, d), x[1].reshape(bs_, ls, d),
            jnp.stack(outs["ckv_p"]), jnp.stack(outs["kr_p"]), jnp.stack(outs["pool_p"]),
            jnp.stack(outs["ckv_s"]), jnp.stack(outs["kr_s"]), jnp.stack(outs["pool_s"]),
            jnp.stack(outs["cmv_s"]))
```

```python
import functools

import jax
import jax.numpy as jnp
from jax import lax
from jax.experimental import pallas as pl
from jax.experimental.pallas import tpu as pltpu

D_MODEL = 4096
DEPTH = 4
CHUNK = 64
EPS = 1e-6
CM_CHUNK = 128
CM_HEADS = 8
CM_HEAD_DIM = 128
D_CM = CM_HEADS * CM_HEAD_DIM
POOL_WINDOWS = (2, 4, 8, 16)
POOL_GROUP_DIM = 256
D_POOL = len(POOL_WINDOWS) * POOL_GROUP_DIM
POOL_HIST = max(POOL_WINDOWS) - 1
HIST_ROWS = POOL_HIST + 1
MLA_HEADS = 16
MLA_Q_RANK = 1024
MLA_KV_RANK = 512
MLA_NOPE_DIM = 128
MLA_ROPE_DIM = 64
MLA_V_DIM = 128
D_MLA = MLA_HEADS * MLA_V_DIM
MLA_SCALE = (MLA_NOPE_DIM + MLA_ROPE_DIM) ** -0.5
Q_SCALE = MLA_SCALE * 1.4426950408889634
ROPE_THETA = 10000.0
Q_HEAD_PAD = 256
OFF_V = D_CM
OFF_POOL = 2 * D_CM
OFF_Q = OFF_POOL + D_POOL
OFF_CKV = OFF_Q + MLA_Q_RANK
OFF_KR = OFF_CKV + MLA_KV_RANK
FF_TILE = 256

LANES = 128
VMEM_LIMIT = 56 * 1024 * 1024

BF16 = jnp.bfloat16
F32 = jnp.float32


def _params(sem):
    return pltpu.CompilerParams(dimension_semantics=sem, vmem_limit_bytes=VMEM_LIMIT)


def _rms(x, g):
    r = lax.rsqrt(jnp.mean(x * x, axis=-1, keepdims=True) + EPS)
    return (x * r) * g


def _dot(a, b):
    return jnp.dot(a, b, preferred_element_type=F32)


def _dot_nt(a, b):
    return lax.dot_general(a, b, (((1,), (1,)), ((), ())), preferred_element_type=F32)


def _rope_lanes(b, c, s):
    return b * c + pltpu.roll(b, 64, axis=1) * s


def _two_source_specs(tm, d, n_first):
    return [pl.BlockSpec((tm, d), lambda i: (jnp.minimum(i, n_first - 1), 0)),
            pl.BlockSpec((tm, d), lambda i: (jnp.maximum(i - n_first, 0), 0))]


def _prenorm_kernel(xp_ref, xs_ref, g_ref, h_ref, *, n_first):
    x = jnp.where(pl.program_id(0) < n_first, xp_ref[...], xs_ref[...])
    h_ref[...] = _rms(x, g_ref[...]).astype(BF16)


def prenorm(xp, xs, g, tm=512):
    d = xp.shape[1]
    m = xp.shape[0] + xs.shape[0]
    n_first = xp.shape[0] // tm
    return pl.pallas_call(
        functools.partial(_prenorm_kernel, n_first=n_first),
        grid=(m // tm,),
        in_specs=_two_source_specs(tm, d, n_first) + [pl.BlockSpec((1, d), lambda i: (0, 0))],
        out_specs=pl.BlockSpec((tm, d), lambda i: (i, 0)),
        out_shape=jax.ShapeDtypeStruct((m, d), BF16),
        compiler_params=_params(("arbitrary",)),
        name="prenorm",
    )(xp, xs, g.reshape(1, d))


def _resnorm_kernel(*refs, n_first, split_in, split_out, with_h):
    refs = list(refs)
    if split_in:
        xp_ref, xs_ref = refs[:2]
        x = jnp.where(pl.program_id(0) < n_first, xp_ref[...], xs_ref[...])
        refs = refs[2:]
    else:
        x = refs[0][...]
        refs = refs[1:]
    f_ref, gpost_ref = refs[:2]
    refs = refs[2:]
    if with_h:
        gpre_ref = refs[0]
        refs = refs[1:]
    xn = x + _rms(f_ref[...].astype(F32), gpost_ref[...])
    if split_out:
        _store_split(xn, refs[0], refs[1], n_first)
        refs = refs[2:]
    else:
        refs[0][...] = xn
        refs = refs[1:]
    if with_h:
        refs[0][...] = _rms(xn, gpre_ref[...]).astype(BF16)


def resnorm(x, f, g_post, g_pre, *, split_rows=None, split_out=False, tm=256):
    split_in = isinstance(x, (tuple, list))
    m, d = f.shape
    with_h = g_pre is not None
    n_first = (x[0].shape[0] if split_in else split_rows if split_out else m) // tm
    row = pl.BlockSpec((tm, d), lambda i: (i, 0))
    vec = pl.BlockSpec((1, d), lambda i: (0, 0))
    in_specs = (_two_source_specs(tm, d, n_first) if split_in else [row]) + [row, vec]
    args = (list(x) if split_in else [x]) + [f, g_post.reshape(1, d)]
    if with_h:
        in_specs.append(vec)
        args.append(g_pre.reshape(1, d))
    if split_out:
        out_specs = _two_source_specs(tm, d, n_first)
        out_shape = [jax.ShapeDtypeStruct((split_rows, d), F32), jax.ShapeDtypeStruct((m - split_rows, d), F32)]
    else:
        out_specs = [row]
        out_shape = [jax.ShapeDtypeStruct((m, d), F32)]
    if with_h:
        out_specs.append(row)
        out_shape.append(jax.ShapeDtypeStruct((m, d), BF16))
    outs = pl.pallas_call(
        functools.partial(_resnorm_kernel, n_first=n_first, split_in=split_in, split_out=split_out, with_h=with_h),
        grid=(m // tm,), in_specs=in_specs, out_specs=out_specs, out_shape=out_shape,
        compiler_params=_params(("arbitrary",)), name="resnorm",
    )(*args)
    xo = tuple(outs[:2]) if split_out else outs[0]
    return xo, (outs[-1] if with_h else None)


def _latnorm_kernel(*refs, n_first, aliased):
    q_ref, c_ref, gq_ref, gc_ref = refs[:4]
    cp_ref, cs_ref, qn_ref, cb_ref = refs[6:] if aliased else refs[4:]
    qn_ref[...] = _rms(q_ref[...], gq_ref[...]).astype(BF16)
    cn = _rms(c_ref[...], gc_ref[...])
    cb_ref[...] = cn.astype(BF16)
    _store_split(cn, cp_ref, cs_ref, n_first)


def _store_split(val, first_ref, second_ref, n_first):
    @pl.when(pl.program_id(0) < n_first)
    def _():
        first_ref[...] = val

    @pl.when(pl.program_id(0) >= n_first)
    def _():
        second_ref[...] = val


def _layer_split_specs(tm, width, l, n_first, n_second):
    return [pl.BlockSpec((tm, width), lambda i, *_: (l * n_first + jnp.minimum(i, n_first - 1), 0)),
            pl.BlockSpec((tm, width), lambda i, *_: (l * n_second + jnp.maximum(i - n_first, 0), 0))]


def _layer_split_io(prev, rows_first, rows_second, width, n_in):
    shapes = [jax.ShapeDtypeStruct((DEPTH * rows_first, width), F32),
              jax.ShapeDtypeStruct((DEPTH * rows_second, width), F32)]
    if prev is None:
        return shapes, [], [], {}
    return shapes, [pl.BlockSpec(memory_space=pl.ANY)] * 2, list(prev), {n_in: 0, n_in + 1: 1}


def latnorm(rest, gq, gc, l, prev, *, rows_first, tm=512):
    m = rest.shape[0]
    qb = (OFF_Q - OFF_POOL) // MLA_Q_RANK
    cb = (OFF_CKV - OFF_POOL) // MLA_KV_RANK
    n_first, n_second = rows_first // tm, (m - rows_first) // tm
    shapes, extra_specs, extra_args, aliases = _layer_split_io(prev, rows_first, m - rows_first, MLA_KV_RANK, 4)
    outs = pl.pallas_call(
        functools.partial(_latnorm_kernel, n_first=n_first, aliased=prev is not None),
        grid=(m // tm,),
        in_specs=[pl.BlockSpec((tm, MLA_Q_RANK), lambda i: (i, qb)),
                  pl.BlockSpec((tm, MLA_KV_RANK), lambda i: (i, cb)),
                  pl.BlockSpec((1, MLA_Q_RANK), lambda i: (0, 0)),
                  pl.BlockSpec((1, MLA_KV_RANK), lambda i: (0, 0))] + extra_specs,
        out_specs=_layer_split_specs(tm, MLA_KV_RANK, l, n_first, n_second)
        + [pl.BlockSpec((tm, MLA_Q_RANK), lambda i: (i, 0)),
           pl.BlockSpec((tm, MLA_KV_RANK), lambda i: (i, 0))],
        out_shape=shapes + [jax.ShapeDtypeStruct((m, MLA_Q_RANK), BF16),
                            jax.ShapeDtypeStruct((m, MLA_KV_RANK), BF16)],
        input_output_aliases=aliases,
        compiler_params=_params(("arbitrary",)), name="latnorm",
    )(rest, rest, gq.reshape(1, -1), gc.reshape(1, -1), *extra_args)
    return outs[2], outs[3], (outs[0], outs[1])


def _mm_kernel(a_ref, w_ref, o_ref, *, act, wt):
    r = (_dot_nt if wt else _dot)(a_ref[...], w_ref[...].astype(BF16))
    if act == "gelu":
        r = jax.nn.gelu(r)
    o_ref[...] = r.astype(o_ref.dtype)


def matmul(a, w, l, *, n_off=0, n=None, act=None, wt=False, out_dtype=F32, tm=1024, tn=512, name="mm"):
    m, k = a.shape
    n = w.shape[1 if wt else 2] if n is None else n
    jo = n_off // tn
    wspec = (pl.BlockSpec((None, tn, k), lambda i, j: (l, j + jo, 0)) if wt
             else pl.BlockSpec((None, k, tn), lambda i, j: (l, 0, j + jo)))
    return pl.pallas_call(
        functools.partial(_mm_kernel, act=act, wt=wt),
        grid=(m // tm, n // tn),
        in_specs=[pl.BlockSpec((tm, k), lambda i, j: (i, 0)), wspec],
        out_specs=pl.BlockSpec((tm, tn), lambda i, j: (i, j)),
        out_shape=jax.ShapeDtypeStruct((m, n), out_dtype),
        compiler_params=_params(("parallel", "arbitrary")), name=name,
    )(a, w)


def _ffn_up_kernel(h_ref, wg_ref, wu_ref, o_ref):
    h = h_ref[...]
    g = _dot(h, wg_ref[...].astype(BF16))
    u = _dot(h, wu_ref[...].astype(BF16))
    o_ref[...] = (jax.nn.silu(g) * u).astype(o_ref.dtype)


def ffn_up(h, wg, wu, l, tm=1536, tn=FF_TILE):
    m, k = h.shape
    n = wg.shape[2]
    wspec = pl.BlockSpec((None, k, tn), lambda i, j: (l, 0, j))
    return pl.pallas_call(
        _ffn_up_kernel,
        grid=(m // tm, n // tn),
        in_specs=[pl.BlockSpec((tm, k), lambda i, j: (i, 0)), wspec, wspec],
        out_specs=pl.BlockSpec((tm, tn), lambda i, j: (i, j)),
        out_shape=jax.ShapeDtypeStruct((m, n), BF16),
        compiler_params=_params(("parallel", "arbitrary")), name="ffn_up",
    )(h, wg, wu)


def _mm_rope_kernel(a_ref, w_ref, c_ref, s_ref, o_ref, *, head_w, scale):
    r = _dot(a_ref[...], w_ref[...])
    if scale is not None:
        r = r * scale
    c = c_ref[...]
    s = s_ref[...]
    for h in range(r.shape[1] // head_w):
        lo = h * head_w
        if head_w > LANES:
            o_ref[:, lo:lo + head_w - LANES] = r[:, lo:lo + head_w - LANES].astype(o_ref.dtype)
        b = r[:, lo + head_w - LANES:lo + head_w]
        o_ref[:, lo + head_w - LANES:lo + head_w] = _rope_lanes(b, c, s).astype(o_ref.dtype)


def matmul_rope(a, w, l, cos_t, sin_t, *, head_w, out_dtype, scale=None, tm=1024, tn=512, name="mm_rope"):
    m, k = a.shape
    n = w.shape[2]
    return pl.pallas_call(
        functools.partial(_mm_rope_kernel, head_w=head_w, scale=scale),
        grid=(m // tm, n // tn),
        in_specs=[pl.BlockSpec((tm, k), lambda i, j: (i, 0)),
                  pl.BlockSpec((None, k, tn), lambda i, j: (l, 0, j)),
                  pl.BlockSpec((tm, LANES), lambda i, j: (i, 0)),
                  pl.BlockSpec((tm, LANES), lambda i, j: (i, 0))],
        out_specs=pl.BlockSpec((tm, tn), lambda i, j: (i, j)),
        out_shape=jax.ShapeDtypeStruct((m, n), out_dtype),
        compiler_params=_params(("parallel", "arbitrary")), name=name,
    )(a, w, cos_t, sin_t)


def _kr_kernel(*refs, n_first, aliased):
    a_ref, w_ref, c_ref, s_ref = refs[:4]
    kp_ref, ks_ref, o_ref = refs[6:] if aliased else refs[4:]
    r = _dot_nt(a_ref[...], w_ref[...].astype(BF16))
    lane = lax.broadcasted_iota(jnp.int32, r.shape, 1)
    half = MLA_ROPE_DIM // 2
    partner = jnp.where(lane < half, pltpu.roll(r, LANES - half, axis=1), pltpu.roll(r, half, axis=1))
    kr = jnp.where(lane < MLA_ROPE_DIM, r * c_ref[...] + partner * s_ref[...], 0.0)
    o_ref[...] = kr
    _store_split(kr[:, :MLA_ROPE_DIM], kp_ref, ks_ref, n_first)


def kr_proj(h, w_in_t, l, cos_t, sin_t, prev, *, rows_first, tm=1024):
    m, k = h.shape
    n_first, n_second = rows_first // tm, (m - rows_first) // tm
    shapes, extra_specs, extra_args, aliases = _layer_split_io(prev, rows_first, m - rows_first, MLA_ROPE_DIM, 4)
    outs = pl.pallas_call(
        functools.partial(_kr_kernel, n_first=n_first, aliased=prev is not None),
        grid=(m // tm,),
        in_specs=[pl.BlockSpec((tm, k), lambda i: (i, 0)),
                  pl.BlockSpec((None, LANES, k), lambda i: (l, OFF_KR // LANES, 0)),
                  pl.BlockSpec((tm, LANES), lambda i: (i, 0)),
                  pl.BlockSpec((tm, LANES), lambda i: (i, 0))] + extra_specs,
        out_specs=_layer_split_specs(tm, MLA_ROPE_DIM, l, n_first, n_second)
        + [pl.BlockSpec((tm, LANES), lambda i: (i, 0))],
        out_shape=shapes + [jax.ShapeDtypeStruct((m, LANES), F32)],
        input_output_aliases=aliases,
        compiler_params=_params(("arbitrary",)), name="in_kr",
    )(h, w_in_t, cos_t, sin_t, *extra_args)
    return outs[2], (outs[0], outs[1])


def _mix_out_kernel(a_ref, b_ref, c_ref, wa_ref, wb_ref, wc_ref, o_ref):
    o_ref[...] = (_dot(a_ref[...], wa_ref[...].astype(BF16)) + _dot(b_ref[...], wb_ref[...].astype(BF16))
                  + _dot(c_ref[...], wc_ref[...].astype(BF16))).astype(o_ref.dtype)


def mix_out(a, b, c, w_out, l, tm=1024, tn=512):
    m = a.shape[0]
    n = w_out.shape[2]
    ka, kb, kc = a.shape[1], b.shape[1], c.shape[1]
    return pl.pallas_call(
        _mix_out_kernel,
        grid=(m // tm, n // tn),
        in_specs=[pl.BlockSpec((tm, ka), lambda i, j: (i, 0)),
                  pl.BlockSpec((tm, kb), lambda i, j: (i, 0)),
                  pl.BlockSpec((tm, kc), lambda i, j: (i, 0)),
                  pl.BlockSpec((None, ka, tn), lambda i, j: (l, 0, j)),
                  pl.BlockSpec((None, kb, tn), lambda i, j: (l, ka // kb, j)),
                  pl.BlockSpec((None, kc, tn), lambda i, j: (l, (ka + kb) // kc, j))],
        out_specs=pl.BlockSpec((tm, tn), lambda i, j: (i, j)),
        out_shape=jax.ShapeDtypeStruct((m, n), BF16),
        compiler_params=_params(("parallel", "arbitrary")), name="mix_out",
    )(a, b, c, w_out, w_out, w_out)


def _chunk_mlp_kernel(u_ref, v_ref, g_ref, b_ref, w_ref, bs_ref, a_ref, vn_ref, *, first_sample_block):
    is_sample = pl.program_id(0) >= first_sample_block
    v = v_ref[...]
    mu = jnp.mean(v, axis=-1, keepdims=True)
    d = v - mu
    var = jnp.mean(d * d, axis=-1, keepdims=True)
    vn = (d * lax.rsqrt(var + EPS)) * g_ref[...] + b_ref[...]
    vn_ref[...] = vn
    vb = vn.astype(BF16)
    ri = lax.broadcasted_iota(jnp.int32, (CM_CHUNK, CM_CHUNK), 0) // CHUNK
    ci = lax.broadcasted_iota(jnp.int32, (CM_CHUNK, CM_CHUNK), 1) // CHUNK
    max_lag = jnp.where(is_sample, 0, 1)
    keep = (ri - ci >= 0) & (ri - ci <= max_lag)
    bs = bs_ref[0]
    for h in range(CM_HEADS):
        cols = slice(h * CM_HEAD_DIM, (h + 1) * CM_HEAD_DIM)
        w = jnp.where(keep, w_ref[0, h], 0.0).astype(BF16)
        sv = _dot(w, vb[:, cols]) + bs[:, h:h + 1]
        a_ref[:, cols] = (u_ref[:, cols] * sv).astype(a_ref.dtype)


def chunk_mlp(uv, ln_g, ln_b, ws2, bs2, l, first_sample_block):
    m = uv.shape[0]
    sel = lambda c: jnp.where(c >= first_sample_block, 1, 0)
    return pl.pallas_call(
        functools.partial(_chunk_mlp_kernel, first_sample_block=first_sample_block),
        grid=(m // CM_CHUNK,),
        in_specs=[pl.BlockSpec((CM_CHUNK, D_CM), lambda c: (c, 0)),
                  pl.BlockSpec((CM_CHUNK, D_CM), lambda c: (c, 1)),
                  pl.BlockSpec((1, D_CM), lambda c: (0, 0)),
                  pl.BlockSpec((1, D_CM), lambda c: (0, 0)),
                  pl.BlockSpec((None, 1, CM_HEADS, CM_CHUNK, CM_CHUNK), lambda c: (l, sel(c), 0, 0, 0)),
                  pl.BlockSpec((None, 1, CM_CHUNK, CM_HEADS), lambda c: (l, sel(c), 0, 0))],
        out_specs=[pl.BlockSpec((CM_CHUNK, D_CM), lambda c: (c, 0)),
                   pl.BlockSpec((CM_CHUNK, D_CM), lambda c: (c, 0))],
        out_shape=[jax.ShapeDtypeStruct((m, D_CM), BF16), jax.ShapeDtypeStruct((m, D_CM), F32)],
        compiler_params=_params(("parallel",)), name="chunk_mlp",
    )(uv, uv, ln_g.reshape(1, -1), ln_b.reshape(1, -1), ws2, bs2)


def _pool_kernel(*refs, t_rows, pos0, aliased):
    if aliased:
        x_ref, hist_ref, w_ref, sc_ref, _, o_ref, e_ref = refs
    else:
        x_ref, hist_ref, w_ref, sc_ref, o_ref, e_ref = refs
    t = pl.program_id(1)

    @pl.when(t == 0)
    def _():
        e_ref[0:HIST_ROWS, :] = hist_ref[0]

    x = x_ref[...]
    e_ref[HIST_ROWS:HIST_ROWS + t_rows, :] = x
    pos = pos0 + t * t_rows + lax.broadcasted_iota(jnp.int32, (t_rows, 1), 0)
    for g, w in enumerate(POOL_WINDOWS):
        cols = slice(g * POOL_GROUP_DIM, (g + 1) * POOL_GROUP_DIM)
        acc = x[:, cols]
        for k in range(1, w):
            acc = acc + e_ref[HIST_ROWS - k:HIST_ROWS - k + t_rows, cols]
        cnt = jnp.minimum(w, pos + 1).astype(F32)
        mg = (acc / cnt - x[:, cols]).astype(BF16)
        y = _dot(mg, w_ref[g]) * sc_ref[:, cols]
        o_ref[:, cols] = y.astype(o_ref.dtype)
    e_ref[0:HIST_ROWS, :] = e_ref[t_rows:t_rows + HIST_ROWS, :]


def pool_mix(rest, hist, hl, pool_w, pool_scale, l, *, row0, n_seq, seq_len, t_rows, pos0, m_total, prev=None):
    n_t = seq_len // t_rows
    b0 = row0 // t_rows
    row_map = lambda s, t: (b0 + s * n_t + t, 0)
    in_specs = [pl.BlockSpec((t_rows, D_POOL), row_map),
                pl.BlockSpec((None, 1, HIST_ROWS, D_POOL), lambda s, t: (hl, s, 0, 0)),
                pl.BlockSpec((None,) + pool_w.shape[1:], lambda s, t: (l, 0, 0, 0)),
                pl.BlockSpec((1, D_POOL), lambda s, t: (0, 0))]
    args = [rest, hist, pool_w, pool_scale.reshape(1, -1)]
    aliases = {}
    if prev is not None:
        in_specs.append(pl.BlockSpec(memory_space=pl.ANY))
        args.append(prev)
        aliases = {4: 0}
    return pl.pallas_call(
        functools.partial(_pool_kernel, t_rows=t_rows, pos0=pos0, aliased=prev is not None),
        grid=(n_seq, n_t),
        in_specs=in_specs,
        out_specs=pl.BlockSpec((t_rows, D_POOL), row_map),
        out_shape=jax.ShapeDtypeStruct((m_total, D_POOL), BF16),
        scratch_shapes=[pltpu.VMEM((HIST_ROWS + t_rows, D_POOL), F32)],
        input_output_aliases=aliases,
        compiler_params=_params(("parallel", "arbitrary")), name="pool_mix",
    )(*args)


KV_HEADS_PER_TILE = 4


def _kv_up_k_kernel(c_ref, w_ref, kr_ref, o_ref):
    r = _dot(c_ref[...], w_ref[...])
    kr = kr_ref[...].astype(BF16)
    for hh in range(KV_HEADS_PER_TILE):
        o_ref[hh, :, 0:MLA_NOPE_DIM] = r[:, hh * MLA_NOPE_DIM:(hh + 1) * MLA_NOPE_DIM].astype(BF16)
        o_ref[hh, :, MLA_NOPE_DIM:Q_HEAD_PAD] = kr


def _kv_up_v_kernel(c_ref, w_ref, o_ref):
    r = _dot(c_ref[...], w_ref[...])
    for hh in range(KV_HEADS_PER_TILE):
        o_ref[hh] = r[:, hh * MLA_V_DIM:(hh + 1) * MLA_V_DIM].astype(BF16)


def kv_up(ckv_b, kr, wkv2, l, rows, tm=1024):
    kd = ckv_b.shape[1]
    tn = KV_HEADS_PER_TILE * MLA_NOPE_DIM
    nj = MLA_HEADS // KV_HEADS_PER_TILE
    k = pl.pallas_call(
        _kv_up_k_kernel,
        grid=(rows // tm, nj),
        in_specs=[pl.BlockSpec((tm, kd), lambda i, j: (i, 0)),
                  pl.BlockSpec((None, kd, tn), lambda i, j: (l, 0, j)),
                  pl.BlockSpec((tm, LANES), lambda i, j: (i, 0))],
        out_specs=pl.BlockSpec((KV_HEADS_PER_TILE, tm, Q_HEAD_PAD), lambda i, j: (j, i, 0)),
        out_shape=jax.ShapeDtypeStruct((MLA_HEADS, rows, Q_HEAD_PAD), BF16),
        compiler_params=_params(("parallel", "arbitrary")), name="kv_up_k",
    )(ckv_b, wkv2, kr)
    v = pl.pallas_call(
        _kv_up_v_kernel,
        grid=(rows // tm, nj),
        in_specs=[pl.BlockSpec((tm, kd), lambda i, j: (i, 0)),
                  pl.BlockSpec((None, kd, tn), lambda i, j: (l, 0, j + nj))],
        out_specs=pl.BlockSpec((KV_HEADS_PER_TILE, tm, MLA_V_DIM), lambda i, j: (j, i, 0)),
        out_shape=jax.ShapeDtypeStruct((MLA_HEADS, rows, MLA_V_DIM), BF16),
        compiler_params=_params(("parallel", "arbitrary")), name="kv_up_v",
    )(ckv_b, wkv2)
    return k, v


def _attn_prompt_kernel(q_ref, k_ref, v_ref, o_ref, qh_ref, m_ref, l_ref, acc_ref, *, t):
    qi = pl.program_id(1)
    ki = pl.program_id(2)

    @pl.when(ki == 0)
    def _():
        for h in range(MLA_HEADS):
            qh_ref[h] = q_ref[:, h * Q_HEAD_PAD:(h + 1) * Q_HEAD_PAD]
        m_ref[...] = jnp.full_like(m_ref, -jnp.inf)
        l_ref[...] = jnp.zeros_like(l_ref)
        acc_ref[...] = jnp.zeros_like(acc_ref)

    def all_heads(bias):
        def head(h, carry):
            s = _dot_nt(qh_ref[h], k_ref[h])
            if bias is not None:
                s = s + bias
            m_prev = m_ref[h]
            m_new = jnp.maximum(m_prev, jnp.max(s, axis=-1, keepdims=True))
            alpha = jnp.exp2(m_prev - m_new)
            p = jnp.exp2(s - jnp.tile(m_new, (1, t // LANES)))
            l_ref[h] = alpha * l_ref[h] + jnp.sum(p, axis=-1, keepdims=True)
            acc_ref[h] = alpha * acc_ref[h] + _dot(p.astype(BF16), v_ref[h])
            m_ref[h] = m_new
            return carry
        lax.fori_loop(0, MLA_HEADS, head, 0, unroll=4)

    @pl.when(ki < qi)
    def _():
        all_heads(None)

    @pl.when(ki == qi)
    def _():
        rc = lax.broadcasted_iota(jnp.int32, (t, t), 0) // CHUNK
        cc = lax.broadcasted_iota(jnp.int32, (t, t), 1) // CHUNK
        all_heads(jnp.where(cc <= rc, 0.0, -jnp.inf))

    @pl.when(ki == pl.num_programs(2) - 1)
    def _():
        for h in range(MLA_HEADS):
            o_ref[:, h * MLA_V_DIM:(h + 1) * MLA_V_DIM] = (acc_ref[h] / l_ref[h]).astype(o_ref.dtype)


def attention_prompt(q, k, v, *, n_batch, seq_len, m_total, t=512):
    nt = seq_len // t
    kmap = lambda b, qi, ki: (0, b * nt + jnp.minimum(ki, qi), 0)
    return pl.pallas_call(
        functools.partial(_attn_prompt_kernel, t=t),
        grid=(n_batch, nt, nt),
        in_specs=[pl.BlockSpec((t, MLA_HEADS * Q_HEAD_PAD), lambda b, qi, ki: (b * nt + qi, 0)),
                  pl.BlockSpec((MLA_HEADS, t, Q_HEAD_PAD), kmap),
                  pl.BlockSpec((MLA_HEADS, t, MLA_V_DIM), kmap)],
        out_specs=pl.BlockSpec((t, D_MLA), lambda b, qi, ki: (b * nt + qi, 0)),
        out_shape=jax.ShapeDtypeStruct((m_total, D_MLA), BF16),
        scratch_shapes=[pltpu.VMEM((MLA_HEADS, t, Q_HEAD_PAD), BF16),
                        pltpu.VMEM((MLA_HEADS, t, LANES), F32),
                        pltpu.VMEM((MLA_HEADS, t, LANES), F32),
                        pltpu.VMEM((MLA_HEADS, t, MLA_V_DIM), F32)],
        compiler_params=_params(("parallel", "parallel", "arbitrary")), name="attn_prompt",
    )(q, k, v)


def _attn_sample_kernel(q_ref, cp_ref, kp_ref, cn_ref, kn_ref, wk_ref, wv_ref, _, o_ref,
                        qa_ref, qr_ref, ol_ref, *, ls):
    q = q_ref[...]
    for h in range(MLA_HEADS):
        rows = slice(h * ls, (h + 1) * ls)
        lo = h * Q_HEAD_PAD
        wk_h = wk_ref[:, h * MLA_NOPE_DIM:(h + 1) * MLA_NOPE_DIM]
        qa_ref[rows, :] = _dot_nt(q[:, lo:lo + MLA_NOPE_DIM], wk_h).astype(BF16)
        qr_ref[rows, :] = q[:, lo + MLA_NOPE_DIM:lo + Q_HEAD_PAD]
    cp = cp_ref[...].astype(BF16)
    kp = kp_ref[...].astype(BF16)
    cn = cn_ref[...]
    kn = kn_ref[...].astype(BF16)
    qa = qa_ref[...]
    qr = qr_ref[...]
    s_p = _dot_nt(qa, cp) + _dot_nt(qr[:, :MLA_ROPE_DIM], kp)
    s_n = _dot_nt(qa, cn) + _dot_nt(qr, kn)
    m = jnp.maximum(jnp.max(s_p, axis=-1, keepdims=True), jnp.max(s_n, axis=-1, keepdims=True))
    p_p = jnp.exp2(s_p - m)
    p_n = jnp.exp2(s_n - m)
    den = jnp.sum(p_p, axis=-1, keepdims=True) + jnp.sum(p_n, axis=-1, keepdims=True)
    ol = (_dot(p_p.astype(BF16), cp) + _dot(p_n.astype(BF16), cn)) / den
    ol_ref[...] = ol.astype(BF16)
    for h in range(MLA_HEADS):
        wv_h = wv_ref[:, h * MLA_V_DIM:(h + 1) * MLA_V_DIM]
        o_ref[:, h * MLA_V_DIM:(h + 1) * MLA_V_DIM] = _dot(ol_ref[h * ls:(h + 1) * ls, :], wv_h).astype(o_ref.dtype)


def attention_sample(q, cache_ckv, cache_kr, ckv_b, kr, wkv2, l, prev, *, row0, n_batch, ls):
    past = cache_ckv.shape[2]
    rb0 = row0 // ls
    rmap = lambda b: (rb0 + b, 0)
    hk = MLA_HEADS * MLA_NOPE_DIM
    return pl.pallas_call(
        functools.partial(_attn_sample_kernel, ls=ls),
        grid=(n_batch,),
        in_specs=[pl.BlockSpec((ls, MLA_HEADS * Q_HEAD_PAD), rmap),
                  pl.BlockSpec((None, None, past, MLA_KV_RANK), lambda b: (l, b, 0, 0)),
                  pl.BlockSpec((None, None, past, MLA_ROPE_DIM), lambda b: (l, b, 0, 0)),
                  pl.BlockSpec((ls, MLA_KV_RANK), rmap),
                  pl.BlockSpec((ls, LANES), rmap),
                  pl.BlockSpec((None, MLA_KV_RANK, hk), lambda b: (l, 0, 0)),
                  pl.BlockSpec((None, MLA_KV_RANK, hk), lambda b: (l, 0, 1)),
                  pl.BlockSpec(memory_space=pl.ANY)],
        out_specs=pl.BlockSpec((ls, D_MLA), rmap),
        out_shape=jax.ShapeDtypeStruct(prev.shape, prev.dtype),
        scratch_shapes=[pltpu.VMEM((MLA_HEADS * ls, MLA_KV_RANK), BF16),
                        pltpu.VMEM((MLA_HEADS * ls, LANES), BF16),
                        pltpu.VMEM((MLA_HEADS * ls, MLA_KV_RANK), BF16)],
        input_output_aliases={7: 0},
        compiler_params=_params(("parallel",)), name="attn_sample",
    )(q, cache_ckv, cache_kr, ckv_b, kr, wkv2, wkv2, prev)


def _rope_tables(pos):
    half = MLA_ROPE_DIM // 2
    inv = ROPE_THETA ** (-jnp.arange(half, dtype=F32) / half)
    ang = pos.astype(F32)[:, None] * inv[None, :]
    cos, sin = jnp.cos(ang), jnp.sin(ang)
    z = jnp.zeros((pos.shape[0], LANES - MLA_ROPE_DIM), F32)
    return jnp.concatenate([cos, cos, z], axis=1), jnp.concatenate([-sin, sin, z], axis=1)


def _swap_halves(w):
    half = w.shape[-1] // 2
    return jnp.concatenate([w[..., half:], w[..., :half]], axis=-1)


def kernel(x_prompt, x_sample, cache_mla_ckv, cache_mla_krope, state_pool, g_pre_mix, g_post_mix, g_pre_ffn, g_post_ffn, w_in, cm_ln_g, cm_ln_b, cm_ws, cm_bs, pool_w, pool_scale, mla_q_norm, mla_w_uq, mla_kv_norm, mla_w_ukv, w_out, ffn_w_gate, ffn_w_up, ffn_w_down):
    bp, lp, d = x_prompt.shape
    bs_, ls, _ = x_sample.shape
    past = cache_mla_ckv.shape[2]
    mp, ms = bp * lp, bs_ * ls
    m = mp + ms

    x = (x_prompt.reshape(mp, d), x_sample.reshape(ms, d))
    pos = jnp.concatenate([jnp.tile(jnp.arange(lp), bp), jnp.tile(past + jnp.arange(ls), bs_)])
    cos_t, sin_t = _rope_tables(pos)

    w_in_t = jnp.swapaxes(w_in, 1, 2)
    wq = mla_w_uq.reshape(DEPTH, MLA_Q_RANK, MLA_HEADS, MLA_NOPE_DIM + MLA_ROPE_DIM)
    wq_rope = wq[..., MLA_NOPE_DIM:]
    wq2 = jnp.concatenate([wq[..., :MLA_NOPE_DIM], wq_rope, _swap_halves(wq_rope)], axis=-1)
    wq2 = wq2.reshape(DEPTH, MLA_Q_RANK, MLA_HEADS * Q_HEAD_PAD).astype(BF16)
    wkv = mla_w_ukv.reshape(DEPTH, MLA_KV_RANK, MLA_HEADS, MLA_NOPE_DIM + MLA_V_DIM)
    wkv2 = jnp.concatenate([wkv[..., :MLA_NOPE_DIM].reshape(DEPTH, MLA_KV_RANK, -1),
                            wkv[..., MLA_NOPE_DIM:].reshape(DEPTH, MLA_KV_RANK, -1)], axis=-1).astype(BF16)
    wd_b = ffn_w_down.astype(BF16)
    pool_w_b = pool_w.astype(BF16)
    ws_s = cm_ws.at[:, :, CHUNK:, CHUNK:].set(cm_ws[:, :, :CHUNK, :CHUNK])
    ws2 = jnp.stack([cm_ws, ws_s], axis=1)
    bs_s = jnp.concatenate([cm_bs[:, :, :CHUNK], cm_bs[:, :, :CHUNK]], axis=-1)
    bs2 = jnp.swapaxes(jnp.stack([cm_bs, bs_s], axis=1), -1, -2)
    hist_p = jnp.zeros((1, bp, HIST_ROWS, D_POOL), F32)
    hist_s = jnp.pad(state_pool, ((0, 0), (0, 0), (HIST_ROWS - POOL_HIST, 0), (0, 0)))

    outs = {k: [] for k in ("pool_p", "pool_s", "cmv_s")}
    ckv_all = kr_all = None
    h = prenorm(x[0], x[1], g_pre_mix[0])
    for l in range(DEPTH):
        last = l + 1 == DEPTH
        uv = matmul(h, w_in_t, l, n_off=0, n=OFF_POOL, act="gelu", wt=True, name="in_uv")
        rest = matmul(h, w_in_t, l, n_off=OFF_POOL, n=OFF_KR - OFF_POOL, wt=True, name="in_rest")
        kr, kr_all = kr_proj(h, w_in_t, l, cos_t, sin_t, kr_all, rows_first=mp)
        a, vn = chunk_mlp(uv, cm_ln_g[l], cm_ln_b[l], ws2, bs2, l, mp // CM_CHUNK)
        b = pool_mix(rest, hist_p, 0, pool_w_b, pool_scale[l], l, row0=0, n_seq=bp, seq_len=lp,
                     t_rows=256, pos0=0, m_total=m)
        b = pool_mix(rest, hist_s, l, pool_w_b, pool_scale[l], l, row0=mp, n_seq=bs_, seq_len=ls,
                     t_rows=ls, pos0=past, m_total=m, prev=b)
        qn, ckv_b, ckv_all = latnorm(rest, mla_q_norm[l], mla_kv_norm[l], l, ckv_all, rows_first=mp)
        q = matmul_rope(qn, wq2, l, cos_t, sin_t, head_w=Q_HEAD_PAD, out_dtype=BF16, scale=Q_SCALE, name="q_up")
        k_p, v_p = kv_up(ckv_b, kr, wkv2, l, mp)
        c = attention_prompt(q, k_p, v_p, n_batch=bp, seq_len=lp, m_total=m)
        c = attention_sample(q, cache_mla_ckv, cache_mla_krope, ckv_b, kr, wkv2, l, c,
                             row0=mp, n_batch=bs_, ls=ls)
        mix = mix_out(a, b, c, w_out, l)
        x, h2 = resnorm(x, mix, g_post_mix[l], g_pre_ffn[l])
        act = ffn_up(h2, ffn_w_gate, ffn_w_up, l)
        f = matmul(act, wd_b, l, out_dtype=BF16, tm=512, tn=512, name="ffn_down")
        x, h = resnorm(x, f, g_post_ffn[l], None if last else g_pre_mix[l + 1],
                       split_rows=mp, split_out=last)

        pool_raw = rest[:, :D_POOL]
        outs["pool_p"].append(pool_raw[:mp].reshape(bp, lp, D_POOL)[:, lp - POOL_HIST:])
        outs["pool_s"].append(pool_raw[mp:].reshape(bs_, ls, D_POOL)[:, ls - POOL_HIST:])
        outs["cmv_s"].append(vn[mp:].reshape(bs_, ls, D_CM))

    return (x[0].reshape(bp, lp, d), x[1].reshape(bs_, ls, d),
            ckv_all[0].reshape(DEPTH, bp, lp, MLA_KV_RANK), kr_all[0].reshape(DEPTH, bp, lp, MLA_ROPE_DIM),
            jnp.stack(outs["pool_p"]),
            ckv_all[1].reshape(DEPTH, bs_, ls, MLA_KV_RANK), kr_all[1].reshape(DEPTH, bs_, ls, MLA_ROPE_DIM),
            jnp.stack(outs["pool_s"]), jnp.stack(outs["cmv_s"]))
```

```python
import functools

import jax
import jax.numpy as jnp
from jax import lax
from jax.experimental import pallas as pl
from jax.experimental.pallas import tpu as pltpu

D_MODEL = 4096
DEPTH = 4
CHUNK = 64
EPS = 1e-6
CM_CHUNK = 128
CM_HEADS = 8
CM_HEAD_DIM = 128
D_CM = CM_HEADS * CM_HEAD_DIM
POOL_WINDOWS = (2, 4, 8, 16)
POOL_GROUP_DIM = 256
D_POOL = len(POOL_WINDOWS) * POOL_GROUP_DIM
POOL_HIST = max(POOL_WINDOWS) - 1
HIST_ROWS = POOL_HIST + 1
MLA_HEADS = 16
MLA_Q_RANK = 1024
MLA_KV_RANK = 512
MLA_NOPE_DIM = 128
MLA_ROPE_DIM = 64
MLA_V_DIM = 128
D_MLA = MLA_HEADS * MLA_V_DIM
MLA_SCALE = (MLA_NOPE_DIM + MLA_ROPE_DIM) ** -0.5
Q_SCALE = MLA_SCALE * 1.4426950408889634
ROPE_THETA = 10000.0
Q_HEAD_PAD = 256
OFF_V = D_CM
OFF_POOL = 2 * D_CM
OFF_Q = OFF_POOL + D_POOL
OFF_CKV = OFF_Q + MLA_Q_RANK
OFF_KR = OFF_CKV + MLA_KV_RANK
FF_TILE = 256

LANES = 128
VMEM_LIMIT = 56 * 1024 * 1024

BF16 = jnp.bfloat16
F32 = jnp.float32


def _params(sem):
    return pltpu.CompilerParams(dimension_semantics=sem, vmem_limit_bytes=VMEM_LIMIT)


def _rms(x, g):
    r = lax.rsqrt(jnp.mean(x * x, axis=-1, keepdims=True) + EPS)
    return (x * r) * g


def _dot(a, b):
    return jnp.dot(a, b, preferred_element_type=F32)


def _dot_nt(a, b):
    return lax.dot_general(a, b, (((1,), (1,)), ((), ())), preferred_element_type=F32)


def _rope_lanes(b, c, s):
    return b * c + pltpu.roll(b, 64, axis=1) * s


def _two_source_specs(tm, d, n_first):
    return [pl.BlockSpec((tm, d), lambda i: (jnp.minimum(i, n_first - 1), 0)),
            pl.BlockSpec((tm, d), lambda i: (jnp.maximum(i - n_first, 0), 0))]


def _prenorm_kernel(xp_ref, xs_ref, g_ref, h_ref, *, n_first):
    x = jnp.where(pl.program_id(0) < n_first, xp_ref[...], xs_ref[...])
    h_ref[...] = _rms(x, g_ref[...]).astype(BF16)


def prenorm(xp, xs, g, tm=512):
    d = xp.shape[1]
    m = xp.shape[0] + xs.shape[0]
    n_first = xp.shape[0] // tm
    return pl.pallas_call(
        functools.partial(_prenorm_kernel, n_first=n_first),
        grid=(m // tm,),
        in_specs=_two_source_specs(tm, d, n_first) + [pl.BlockSpec((1, d), lambda i: (0, 0))],
        out_specs=pl.BlockSpec((tm, d), lambda i: (i, 0)),
        out_shape=jax.ShapeDtypeStruct((m, d), BF16),
        compiler_params=_params(("arbitrary",)),
        name="prenorm",
    )(xp, xs, g.reshape(1, d))


def _resnorm_kernel(*refs, n_first, split_in, split_out, with_h):
    refs = list(refs)
    if split_in:
        xp_ref, xs_ref = refs[:2]
        x = jnp.where(pl.program_id(0) < n_first, xp_ref[...], xs_ref[...])
        refs = refs[2:]
    else:
        x = refs[0][...]
        refs = refs[1:]
    f_ref, gpost_ref = refs[:2]
    refs = refs[2:]
    if with_h:
        gpre_ref = refs[0]
        refs = refs[1:]
    xn = x + _rms(f_ref[...].astype(F32), gpost_ref[...])
    if split_out:
        _store_split(xn, refs[0], refs[1], n_first)
        refs = refs[2:]
    else:
        refs[0][...] = xn
        refs = refs[1:]
    if with_h:
        refs[0][...] = _rms(xn, gpre_ref[...]).astype(BF16)


def resnorm(x, f, g_post, g_pre, *, split_rows=None, split_out=False):
    split_in = isinstance(x, (tuple, list))
    tm = 256 if split_in or split_out else 384
    m, d = f.shape
    with_h = g_pre is not None
    n_first = (x[0].shape[0] if split_in else split_rows if split_out else m) // tm
    row = pl.BlockSpec((tm, d), lambda i: (i, 0))
    vec = pl.BlockSpec((1, d), lambda i: (0, 0))
    in_specs = (_two_source_specs(tm, d, n_first) if split_in else [row]) + [row, vec]
    args = (list(x) if split_in else [x]) + [f, g_post.reshape(1, d)]
    if with_h:
        in_specs.append(vec)
        args.append(g_pre.reshape(1, d))
    if split_out:
        out_specs = _two_source_specs(tm, d, n_first)
        out_shape = [jax.ShapeDtypeStruct((split_rows, d), F32), jax.ShapeDtypeStruct((m - split_rows, d), F32)]
    else:
        out_specs = [row]
        out_shape = [jax.ShapeDtypeStruct((m, d), F32)]
    if with_h:
        out_specs.append(row)
        out_shape.append(jax.ShapeDtypeStruct((m, d), BF16))
    outs = pl.pallas_call(
        functools.partial(_resnorm_kernel, n_first=n_first, split_in=split_in, split_out=split_out, with_h=with_h),
        grid=(m // tm,), in_specs=in_specs, out_specs=out_specs, out_shape=out_shape,
        compiler_params=_params(("arbitrary",)), name="resnorm",
    )(*args)
    xo = tuple(outs[:2]) if split_out else outs[0]
    return xo, (outs[-1] if with_h else None)


def _latnorm_kernel(*refs, n_first, aliased):
    q_ref, c_ref, gq_ref, gc_ref = refs[:4]
    cp_ref, cs_ref, qn_ref, cb_ref = refs[6:] if aliased else refs[4:]
    qn_ref[...] = _rms(q_ref[...], gq_ref[...]).astype(BF16)
    cn = _rms(c_ref[...], gc_ref[...])
    cb_ref[...] = cn.astype(BF16)
    _store_split(cn, cp_ref, cs_ref, n_first)


def _store_split(val, first_ref, second_ref, n_first):
    @pl.when(pl.program_id(0) < n_first)
    def _():
        first_ref[...] = val

    @pl.when(pl.program_id(0) >= n_first)
    def _():
        second_ref[...] = val


def _layer_split_specs(tm, width, l, n_first, n_second):
    return [pl.BlockSpec((tm, width), lambda i, *_: (l * n_first + jnp.minimum(i, n_first - 1), 0)),
            pl.BlockSpec((tm, width), lambda i, *_: (l * n_second + jnp.maximum(i - n_first, 0), 0))]


def _layer_split_io(prev, rows_first, rows_second, width, n_in):
    shapes = [jax.ShapeDtypeStruct((DEPTH * rows_first, width), F32),
              jax.ShapeDtypeStruct((DEPTH * rows_second, width), F32)]
    if prev is None:
        return shapes, [], [], {}
    return shapes, [pl.BlockSpec(memory_space=pl.ANY)] * 2, list(prev), {n_in: 0, n_in + 1: 1}


def latnorm(rest, gq, gc, l, prev, *, rows_first, tm=512):
    m = rest.shape[0]
    qb = (OFF_Q - OFF_POOL) // MLA_Q_RANK
    cb = (OFF_CKV - OFF_POOL) // MLA_KV_RANK
    n_first, n_second = rows_first // tm, (m - rows_first) // tm
    shapes, extra_specs, extra_args, aliases = _layer_split_io(prev, rows_first, m - rows_first, MLA_KV_RANK, 4)
    outs = pl.pallas_call(
        functools.partial(_latnorm_kernel, n_first=n_first, aliased=prev is not None),
        grid=(m // tm,),
        in_specs=[pl.BlockSpec((tm, MLA_Q_RANK), lambda i: (i, qb)),
                  pl.BlockSpec((tm, MLA_KV_RANK), lambda i: (i, cb)),
                  pl.BlockSpec((1, MLA_Q_RANK), lambda i: (0, 0)),
                  pl.BlockSpec((1, MLA_KV_RANK), lambda i: (0, 0))] + extra_specs,
        out_specs=_layer_split_specs(tm, MLA_KV_RANK, l, n_first, n_second)
        + [pl.BlockSpec((tm, MLA_Q_RANK), lambda i: (i, 0)),
           pl.BlockSpec((tm, MLA_KV_RANK), lambda i: (i, 0))],
        out_shape=shapes + [jax.ShapeDtypeStruct((m, MLA_Q_RANK), BF16),
                            jax.ShapeDtypeStruct((m, MLA_KV_RANK), BF16)],
        input_output_aliases=aliases,
        compiler_params=_params(("arbitrary",)), name="latnorm",
    )(rest, rest, gq.reshape(1, -1), gc.reshape(1, -1), *extra_args)
    return outs[2], outs[3], (outs[0], outs[1])


def _mm_kernel(a_ref, w_ref, o_ref, *, act, wt):
    r = (_dot_nt if wt else _dot)(a_ref[...], w_ref[...].astype(BF16))
    if act == "gelu":
        r = jax.nn.gelu(r)
    o_ref[...] = r.astype(o_ref.dtype)


def matmul(a, w, l, *, n_off=0, n=None, act=None, wt=False, out_dtype=F32, tm=1024, tn=512, name="mm"):
    m, k = a.shape
    n = w.shape[1 if wt else 2] if n is None else n
    jo = n_off // tn
    wspec = (pl.BlockSpec((None, tn, k), lambda i, j: (l, j + jo, 0)) if wt
             else pl.BlockSpec((None, k, tn), lambda i, j: (l, 0, j + jo)))
    return pl.pallas_call(
        functools.partial(_mm_kernel, act=act, wt=wt),
        grid=(m // tm, n // tn),
        in_specs=[pl.BlockSpec((tm, k), lambda i, j: (i, 0)), wspec],
        out_specs=pl.BlockSpec((tm, tn), lambda i, j: (i, j)),
        out_shape=jax.ShapeDtypeStruct((m, n), out_dtype),
        compiler_params=_params(("parallel", "arbitrary")), name=name,
    )(a, w)


def _ffn_up_kernel(h_ref, wg_ref, wu_ref, o_ref):
    h = h_ref[...]
    g = _dot(h, wg_ref[...].astype(BF16))
    u = _dot(h, wu_ref[...].astype(BF16))
    o_ref[...] = (jax.nn.silu(g) * u).astype(o_ref.dtype)


def ffn_up(h, wg, wu, l, tm=1536, tn=FF_TILE):
    m, k = h.shape
    n = wg.shape[2]
    wspec = pl.BlockSpec((None, k, tn), lambda i, j: (l, 0, j))
    return pl.pallas_call(
        _ffn_up_kernel,
        grid=(m // tm, n // tn),
        in_specs=[pl.BlockSpec((tm, k), lambda i, j: (i, 0)), wspec, wspec],
        out_specs=pl.BlockSpec((tm, tn), lambda i, j: (i, j)),
        out_shape=jax.ShapeDtypeStruct((m, n), BF16),
        compiler_params=_params(("parallel", "arbitrary")), name="ffn_up",
    )(h, wg, wu)


def _mm_rope_kernel(a_ref, w_ref, c_ref, s_ref, o_ref, *, head_w, scale):
    r = _dot(a_ref[...], w_ref[...])
    if scale is not None:
        r = r * scale
    c = c_ref[...]
    s = s_ref[...]
    for h in range(r.shape[1] // head_w):
        lo = h * head_w
        if head_w > LANES:
            o_ref[:, lo:lo + head_w - LANES] = r[:, lo:lo + head_w - LANES].astype(o_ref.dtype)
        b = r[:, lo + head_w - LANES:lo + head_w]
        o_ref[:, lo + head_w - LANES:lo + head_w] = _rope_lanes(b, c, s).astype(o_ref.dtype)


def matmul_rope(a, w, l, cos_t, sin_t, *, head_w, out_dtype, scale=None, tm=1024, tn=512, name="mm_rope"):
    m, k = a.shape
    n = w.shape[2]
    return pl.pallas_call(
        functools.partial(_mm_rope_kernel, head_w=head_w, scale=scale),
        grid=(m // tm, n // tn),
        in_specs=[pl.BlockSpec((tm, k), lambda i, j: (i, 0)),
                  pl.BlockSpec((None, k, tn), lambda i, j: (l, 0, j)),
                  pl.BlockSpec((tm, LANES), lambda i, j: (i, 0)),
                  pl.BlockSpec((tm, LANES), lambda i, j: (i, 0))],
        out_specs=pl.BlockSpec((tm, tn), lambda i, j: (i, j)),
        out_shape=jax.ShapeDtypeStruct((m, n), out_dtype),
        compiler_params=_params(("parallel", "arbitrary")), name=name,
    )(a, w, cos_t, sin_t)


def _kr_kernel(*refs, n_first, aliased):
    a_ref, w_ref, c_ref, s_ref = refs[:4]
    kp_ref, ks_ref, o_ref = refs[6:] if aliased else refs[4:]
    r = _dot_nt(a_ref[...], w_ref[...].astype(BF16))
    lane = lax.broadcasted_iota(jnp.int32, r.shape, 1)
    half = MLA_ROPE_DIM // 2
    partner = jnp.where(lane < half, pltpu.roll(r, LANES - half, axis=1), pltpu.roll(r, half, axis=1))
    kr = jnp.where(lane < MLA_ROPE_DIM, r * c_ref[...] + partner * s_ref[...], 0.0)
    o_ref[...] = kr
    _store_split(kr[:, :MLA_ROPE_DIM], kp_ref, ks_ref, n_first)


def kr_proj(h, w_in_t, l, cos_t, sin_t, prev, *, rows_first, tm=1024):
    m, k = h.shape
    n_first, n_second = rows_first // tm, (m - rows_first) // tm
    shapes, extra_specs, extra_args, aliases = _layer_split_io(prev, rows_first, m - rows_first, MLA_ROPE_DIM, 4)
    outs = pl.pallas_call(
        functools.partial(_kr_kernel, n_first=n_first, aliased=prev is not None),
        grid=(m // tm,),
        in_specs=[pl.BlockSpec((tm, k), lambda i: (i, 0)),
                  pl.BlockSpec((None, LANES, k), lambda i: (l, OFF_KR // LANES, 0)),
                  pl.BlockSpec((tm, LANES), lambda i: (i, 0)),
                  pl.BlockSpec((tm, LANES), lambda i: (i, 0))] + extra_specs,
        out_specs=_layer_split_specs(tm, MLA_ROPE_DIM, l, n_first, n_second)
        + [pl.BlockSpec((tm, LANES), lambda i: (i, 0))],
        out_shape=shapes + [jax.ShapeDtypeStruct((m, LANES), F32)],
        input_output_aliases=aliases,
        compiler_params=_params(("arbitrary",)), name="in_kr",
    )(h, w_in_t, cos_t, sin_t, *extra_args)
    return outs[2], (outs[0], outs[1])


def _mix_out_kernel(a_ref, b_ref, c_ref, wa_ref, wb_ref, wc_ref, o_ref):
    o_ref[...] = (_dot(a_ref[...], wa_ref[...].astype(BF16)) + _dot(b_ref[...], wb_ref[...].astype(BF16))
                  + _dot(c_ref[...], wc_ref[...].astype(BF16))).astype(o_ref.dtype)


def mix_out(a, b, c, w_out, l, tm=1024, tn=512):
    m = a.shape[0]
    n = w_out.shape[2]
    ka, kb, kc = a.shape[1], b.shape[1], c.shape[1]
    return pl.pallas_call(
        _mix_out_kernel,
        grid=(m // tm, n // tn),
        in_specs=[pl.BlockSpec((tm, ka), lambda i, j: (i, 0)),
                  pl.BlockSpec((tm, kb), lambda i, j: (i, 0)),
                  pl.BlockSpec((tm, kc), lambda i, j: (i, 0)),
                  pl.BlockSpec((None, ka, tn), lambda i, j: (l, 0, j)),
                  pl.BlockSpec((None, kb, tn), lambda i, j: (l, ka // kb, j)),
                  pl.BlockSpec((None, kc, tn), lambda i, j: (l, (ka + kb) // kc, j))],
        out_specs=pl.BlockSpec((tm, tn), lambda i, j: (i, j)),
        out_shape=jax.ShapeDtypeStruct((m, n), BF16),
        compiler_params=_params(("parallel", "arbitrary")), name="mix_out",
    )(a, b, c, w_out, w_out, w_out)


def _chunk_mlp_kernel(*refs, first_sample_block, aliased):
    u_ref, v_ref, g_ref, b_ref, w_ref, bs_ref = refs[:6]
    a_ref, vn_ref = refs[7:] if aliased else refs[6:]
    is_sample = pl.program_id(0) >= first_sample_block
    ri = lax.broadcasted_iota(jnp.int32, (CM_CHUNK, CM_CHUNK), 0) // CHUNK
    ci = lax.broadcasted_iota(jnp.int32, (CM_CHUNK, CM_CHUNK), 1) // CHUNK
    max_lag = jnp.where(is_sample, 0, 1)
    keep = (ri - ci >= 0) & (ri - ci <= max_lag)
    bs = bs_ref[0]
    ws = [jnp.where(keep, w_ref[0, h], 0.0).astype(BF16) for h in range(CM_HEADS)]
    for c in range(u_ref.shape[0] // CM_CHUNK):
        rows = slice(c * CM_CHUNK, (c + 1) * CM_CHUNK)
        v = v_ref[rows, :]
        mu = jnp.mean(v, axis=-1, keepdims=True)
        d = v - mu
        var = jnp.mean(d * d, axis=-1, keepdims=True)
        vn = (d * lax.rsqrt(var + EPS)) * g_ref[...] + b_ref[...]
        @pl.when(is_sample)
        def _():
            vn_ref[rows, :] = vn

        vb = vn.astype(BF16)
        for h in range(CM_HEADS):
            cols = slice(h * CM_HEAD_DIM, (h + 1) * CM_HEAD_DIM)
            sv = _dot(ws[h], vb[:, cols]) + bs[:, h:h + 1]
            a_ref[rows, cols] = (u_ref[rows, cols] * sv).astype(a_ref.dtype)


CM_ROWS = 4 * CM_CHUNK


def chunk_mlp(uv, ln_g, ln_b, ws2, bs2, l, first_sample_row, prev):
    m = uv.shape[0]
    first_sample_block = first_sample_row // CM_ROWS
    n_sample = m // CM_ROWS - first_sample_block
    sel = lambda c: jnp.where(c >= first_sample_block, 1, 0)
    extra_specs, extra_args, aliases = [], [], {}
    if prev is not None:
        extra_specs, extra_args, aliases = [pl.BlockSpec(memory_space=pl.ANY)], [prev], {6: 1}
    return pl.pallas_call(
        functools.partial(_chunk_mlp_kernel, first_sample_block=first_sample_block, aliased=prev is not None),
        grid=(m // CM_ROWS,),
        in_specs=[pl.BlockSpec((CM_ROWS, D_CM), lambda c: (c, 0)),
                  pl.BlockSpec((CM_ROWS, D_CM), lambda c: (c, 1)),
                  pl.BlockSpec((1, D_CM), lambda c: (0, 0)),
                  pl.BlockSpec((1, D_CM), lambda c: (0, 0)),
                  pl.BlockSpec((None, 1, CM_HEADS, CM_CHUNK, CM_CHUNK), lambda c: (l, sel(c), 0, 0, 0)),
                  pl.BlockSpec((None, 1, CM_CHUNK, CM_HEADS), lambda c: (l, sel(c), 0, 0))] + extra_specs,
        out_specs=[pl.BlockSpec((CM_ROWS, D_CM), lambda c: (c, 0)),
                   pl.BlockSpec((CM_ROWS, D_CM),
                                lambda c: (l * n_sample + jnp.maximum(c - first_sample_block, 0), 0))],
        out_shape=[jax.ShapeDtypeStruct((m, D_CM), BF16),
                   jax.ShapeDtypeStruct((DEPTH * n_sample * CM_ROWS, D_CM), F32)],
        input_output_aliases=aliases,
        compiler_params=_params(("arbitrary",)), name="chunk_mlp",
    )(uv, uv, ln_g.reshape(1, -1), ln_b.reshape(1, -1), ws2, bs2, *extra_args)


def _pool_kernel(*refs, t_rows, pos0, aliased):
    if aliased:
        x_ref, hist_ref, w_ref, sc_ref, _, o_ref, e_ref = refs
    else:
        x_ref, hist_ref, w_ref, sc_ref, o_ref, e_ref = refs
    t = pl.program_id(1)

    @pl.when(t == 0)
    def _():
        e_ref[0:HIST_ROWS, :] = hist_ref[0]

    x = x_ref[...]
    e_ref[HIST_ROWS:HIST_ROWS + t_rows, :] = x
    pos = pos0 + t * t_rows + lax.broadcasted_iota(jnp.int32, (t_rows, 1), 0)
    for g, w in enumerate(POOL_WINDOWS):
        cols = slice(g * POOL_GROUP_DIM, (g + 1) * POOL_GROUP_DIM)
        acc = x[:, cols]
        for k in range(1, w):
            acc = acc + e_ref[HIST_ROWS - k:HIST_ROWS - k + t_rows, cols]
        cnt = jnp.minimum(w, pos + 1).astype(F32)
        mg = (acc / cnt - x[:, cols]).astype(BF16)
        y = _dot(mg, w_ref[g]) * sc_ref[:, cols]
        o_ref[:, cols] = y.astype(o_ref.dtype)
    e_ref[0:HIST_ROWS, :] = e_ref[t_rows:t_rows + HIST_ROWS, :]


def pool_mix(rest, hist, hl, pool_w, pool_scale, l, *, row0, n_seq, seq_len, t_rows, pos0, m_total, prev=None):
    n_t = seq_len // t_rows
    b0 = row0 // t_rows
    row_map = lambda s, t: (b0 + s * n_t + t, 0)
    in_specs = [pl.BlockSpec((t_rows, D_POOL), row_map),
                pl.BlockSpec((None, 1, HIST_ROWS, D_POOL), lambda s, t: (hl, s, 0, 0)),
                pl.BlockSpec((None,) + pool_w.shape[1:], lambda s, t: (l, 0, 0, 0)),
                pl.BlockSpec((1, D_POOL), lambda s, t: (0, 0))]
    args = [rest, hist, pool_w, pool_scale.reshape(1, -1)]
    aliases = {}
    if prev is not None:
        in_specs.append(pl.BlockSpec(memory_space=pl.ANY))
        args.append(prev)
        aliases = {4: 0}
    return pl.pallas_call(
        functools.partial(_pool_kernel, t_rows=t_rows, pos0=pos0, aliased=prev is not None),
        grid=(n_seq, n_t),
        in_specs=in_specs,
        out_specs=pl.BlockSpec((t_rows, D_POOL), row_map),
        out_shape=jax.ShapeDtypeStruct((m_total, D_POOL), BF16),
        scratch_shapes=[pltpu.VMEM((HIST_ROWS + t_rows, D_POOL), F32)],
        input_output_aliases=aliases,
        compiler_params=_params(("parallel", "arbitrary")), name="pool_mix",
    )(*args)


KV_HEADS_PER_TILE = 4


def _kv_up_k_kernel(c_ref, w_ref, kr_ref, o_ref):
    r = _dot(c_ref[...], w_ref[...])
    kr = kr_ref[...].astype(BF16)
    for hh in range(KV_HEADS_PER_TILE):
        o_ref[hh, :, 0:MLA_NOPE_DIM] = r[:, hh * MLA_NOPE_DIM:(hh + 1) * MLA_NOPE_DIM].astype(BF16)
        o_ref[hh, :, MLA_NOPE_DIM:Q_HEAD_PAD] = kr


def _kv_up_v_kernel(c_ref, w_ref, o_ref):
    r = _dot(c_ref[...], w_ref[...])
    for hh in range(KV_HEADS_PER_TILE):
        o_ref[hh] = r[:, hh * MLA_V_DIM:(hh + 1) * MLA_V_DIM].astype(BF16)


def kv_up(ckv_b, kr, wkv2, l, rows, tm=1024):
    kd = ckv_b.shape[1]
    tn = KV_HEADS_PER_TILE * MLA_NOPE_DIM
    nj = MLA_HEADS // KV_HEADS_PER_TILE
    k = pl.pallas_call(
        _kv_up_k_kernel,
        grid=(rows // tm, nj),
        in_specs=[pl.BlockSpec((tm, kd), lambda i, j: (i, 0)),
                  pl.BlockSpec((None, kd, tn), lambda i, j: (l, 0, j)),
                  pl.BlockSpec((tm, LANES), lambda i, j: (i, 0))],
        out_specs=pl.BlockSpec((KV_HEADS_PER_TILE, tm, Q_HEAD_PAD), lambda i, j: (j, i, 0)),
        out_shape=jax.ShapeDtypeStruct((MLA_HEADS, rows, Q_HEAD_PAD), BF16),
        compiler_params=_params(("parallel", "arbitrary")), name="kv_up_k",
    )(ckv_b, wkv2, kr)
    v = pl.pallas_call(
        _kv_up_v_kernel,
        grid=(rows // tm, nj),
        in_specs=[pl.BlockSpec((tm, kd), lambda i, j: (i, 0)),
                  pl.BlockSpec((None, kd, tn), lambda i, j: (l, 0, j + nj))],
        out_specs=pl.BlockSpec((KV_HEADS_PER_TILE, tm, MLA_V_DIM), lambda i, j: (j, i, 0)),
        out_shape=jax.ShapeDtypeStruct((MLA_HEADS, rows, MLA_V_DIM), BF16),
        compiler_params=_params(("parallel", "arbitrary")), name="kv_up_v",
    )(ckv_b, wkv2)
    return k, v


def _attn_prompt_kernel(q_ref, k_ref, v_ref, o_ref, qh_ref, m_ref, l_ref, acc_ref, *, t):
    qi = pl.program_id(1)
    ki = pl.program_id(2)

    @pl.when(ki == 0)
    def _():
        for h in range(MLA_HEADS):
            qh_ref[h] = q_ref[:, h * Q_HEAD_PAD:(h + 1) * Q_HEAD_PAD]
        m_ref[...] = jnp.full_like(m_ref, -jnp.inf)
        l_ref[...] = jnp.zeros_like(l_ref)
        acc_ref[...] = jnp.zeros_like(acc_ref)

    def all_heads(bias):
        def head(h, carry):
            s = _dot_nt(qh_ref[h], k_ref[h])
            if bias is not None:
                s = s + bias
            m_prev = m_ref[h]
            m_new = jnp.maximum(m_prev, jnp.max(s, axis=-1, keepdims=True))
            alpha = jnp.exp2(m_prev - m_new)
            p = jnp.exp2(s - jnp.tile(m_new, (1, t // LANES)))
            l_ref[h] = alpha * l_ref[h] + jnp.sum(p, axis=-1, keepdims=True)
            acc_ref[h] = alpha * acc_ref[h] + _dot(p.astype(BF16), v_ref[h])
            m_ref[h] = m_new
            return carry
        lax.fori_loop(0, MLA_HEADS, head, 0, unroll=4)

    @pl.when(ki < qi)
    def _():
        all_heads(None)

    @pl.when(ki == qi)
    def _():
        rc = lax.broadcasted_iota(jnp.int32, (t, t), 0) // CHUNK
        cc = lax.broadcasted_iota(jnp.int32, (t, t), 1) // CHUNK
        all_heads(jnp.where(cc <= rc, 0.0, -jnp.inf))

    @pl.when(ki == pl.num_programs(2) - 1)
    def _():
        for h in range(MLA_HEADS):
            o_ref[:, h * MLA_V_DIM:(h + 1) * MLA_V_DIM] = (acc_ref[h] / l_ref[h]).astype(o_ref.dtype)


def attention_prompt(q, k, v, *, n_batch, seq_len, m_total, t=512):
    nt = seq_len // t
    kmap = lambda b, qi, ki: (0, b * nt + jnp.minimum(ki, qi), 0)
    return pl.pallas_call(
        functools.partial(_attn_prompt_kernel, t=t),
        grid=(n_batch, nt, nt),
        in_specs=[pl.BlockSpec((t, MLA_HEADS * Q_HEAD_PAD), lambda b, qi, ki: (b * nt + qi, 0)),
                  pl.BlockSpec((MLA_HEADS, t, Q_HEAD_PAD), kmap),
                  pl.BlockSpec((MLA_HEADS, t, MLA_V_DIM), kmap)],
        out_specs=pl.BlockSpec((t, D_MLA), lambda b, qi, ki: (b * nt + qi, 0)),
        out_shape=jax.ShapeDtypeStruct((m_total, D_MLA), BF16),
        scratch_shapes=[pltpu.VMEM((MLA_HEADS, t, Q_HEAD_PAD), BF16),
                        pltpu.VMEM((MLA_HEADS, t, LANES), F32),
                        pltpu.VMEM((MLA_HEADS, t, LANES), F32),
                        pltpu.VMEM((MLA_HEADS, t, MLA_V_DIM), F32)],
        compiler_params=_params(("parallel", "parallel", "arbitrary")), name="attn_prompt",
    )(q, k, v)


def _attn_sample_kernel(q_ref, cp_ref, kp_ref, cn_ref, kn_ref, wk_ref, wv_ref, _, o_ref,
                        qa_ref, qr_ref, ol_ref, *, ls):
    q = q_ref[...]
    for h in range(MLA_HEADS):
        rows = slice(h * ls, (h + 1) * ls)
        lo = h * Q_HEAD_PAD
        wk_h = wk_ref[:, h * MLA_NOPE_DIM:(h + 1) * MLA_NOPE_DIM]
        qa_ref[rows, :] = _dot_nt(q[:, lo:lo + MLA_NOPE_DIM], wk_h).astype(BF16)
        qr_ref[rows, :] = q[:, lo + MLA_NOPE_DIM:lo + Q_HEAD_PAD]
    cp = cp_ref[...].astype(BF16)
    kp = kp_ref[...].astype(BF16)
    cn = cn_ref[...]
    kn = kn_ref[...].astype(BF16)
    qa = qa_ref[...]
    qr = qr_ref[...]
    s_p = _dot_nt(qa, cp) + _dot_nt(qr[:, :MLA_ROPE_DIM], kp)
    s_n = _dot_nt(qa, cn) + _dot_nt(qr, kn)
    m = jnp.maximum(jnp.max(s_p, axis=-1, keepdims=True), jnp.max(s_n, axis=-1, keepdims=True))
    p_p = jnp.exp2(s_p - m)
    p_n = jnp.exp2(s_n - m)
    den = jnp.sum(p_p, axis=-1, keepdims=True) + jnp.sum(p_n, axis=-1, keepdims=True)
    ol = (_dot(p_p.astype(BF16), cp) + _dot(p_n.astype(BF16), cn)) / den
    ol_ref[...] = ol.astype(BF16)
    for h in range(MLA_HEADS):
        wv_h = wv_ref[:, h * MLA_V_DIM:(h + 1) * MLA_V_DIM]
        o_ref[:, h * MLA_V_DIM:(h + 1) * MLA_V_DIM] = _dot(ol_ref[h * ls:(h + 1) * ls, :], wv_h).astype(o_ref.dtype)


def attention_sample(q, cache_ckv, cache_kr, ckv_b, kr, wkv2, l, prev, *, row0, n_batch, ls):
    past = cache_ckv.shape[2]
    rb0 = row0 // ls
    rmap = lambda b: (rb0 + b, 0)
    hk = MLA_HEADS * MLA_NOPE_DIM
    return pl.pallas_call(
        functools.partial(_attn_sample_kernel, ls=ls),
        grid=(n_batch,),
        in_specs=[pl.BlockSpec((ls, MLA_HEADS * Q_HEAD_PAD), rmap),
                  pl.BlockSpec((None, None, past, MLA_KV_RANK), lambda b: (l, b, 0, 0)),
                  pl.BlockSpec((None, None, past, MLA_ROPE_DIM), lambda b: (l, b, 0, 0)),
                  pl.BlockSpec((ls, MLA_KV_RANK), rmap),
                  pl.BlockSpec((ls, LANES), rmap),
                  pl.BlockSpec((None, MLA_KV_RANK, hk), lambda b: (l, 0, 0)),
                  pl.BlockSpec((None, MLA_KV_RANK, hk), lambda b: (l, 0, 1)),
                  pl.BlockSpec(memory_space=pl.ANY)],
        out_specs=pl.BlockSpec((ls, D_MLA), rmap),
        out_shape=jax.ShapeDtypeStruct(prev.shape, prev.dtype),
        scratch_shapes=[pltpu.VMEM((MLA_HEADS * ls, MLA_KV_RANK), BF16),
                        pltpu.VMEM((MLA_HEADS * ls, LANES), BF16),
                        pltpu.VMEM((MLA_HEADS * ls, MLA_KV_RANK), BF16)],
        input_output_aliases={7: 0},
        compiler_params=_params(("parallel",)), name="attn_sample",
    )(q, cache_ckv, cache_kr, ckv_b, kr, wkv2, wkv2, prev)


def _rope_tables(pos):
    half = MLA_ROPE_DIM // 2
    inv = ROPE_THETA ** (-jnp.arange(half, dtype=F32) / half)
    ang = pos.astype(F32)[:, None] * inv[None, :]
    cos, sin = jnp.cos(ang), jnp.sin(ang)
    z = jnp.zeros((pos.shape[0], LANES - MLA_ROPE_DIM), F32)
    return jnp.concatenate([cos, cos, z], axis=1), jnp.concatenate([-sin, sin, z], axis=1)


def _swap_halves(w):
    half = w.shape[-1] // 2
    return jnp.concatenate([w[..., half:], w[..., :half]], axis=-1)


def kernel(x_prompt, x_sample, cache_mla_ckv, cache_mla_krope, state_pool, g_pre_mix, g_post_mix, g_pre_ffn, g_post_ffn, w_in, cm_ln_g, cm_ln_b, cm_ws, cm_bs, pool_w, pool_scale, mla_q_norm, mla_w_uq, mla_kv_norm, mla_w_ukv, w_out, ffn_w_gate, ffn_w_up, ffn_w_down):
    bp, lp, d = x_prompt.shape
    bs_, ls, _ = x_sample.shape
    past = cache_mla_ckv.shape[2]
    mp, ms = bp * lp, bs_ * ls
    m = mp + ms

    x = (x_prompt.reshape(mp, d), x_sample.reshape(ms, d))
    pos = jnp.concatenate([jnp.tile(jnp.arange(lp), bp), jnp.tile(past + jnp.arange(ls), bs_)])
    cos_t, sin_t = _rope_tables(pos)

    w_in_t = jnp.swapaxes(w_in, 1, 2)
    wq = mla_w_uq.reshape(DEPTH, MLA_Q_RANK, MLA_HEADS, MLA_NOPE_DIM + MLA_ROPE_DIM)
    wq_rope = wq[..., MLA_NOPE_DIM:]
    wq2 = jnp.concatenate([wq[..., :MLA_NOPE_DIM], wq_rope, _swap_halves(wq_rope)], axis=-1)
    wq2 = wq2.reshape(DEPTH, MLA_Q_RANK, MLA_HEADS * Q_HEAD_PAD).astype(BF16)
    wkv = mla_w_ukv.reshape(DEPTH, MLA_KV_RANK, MLA_HEADS, MLA_NOPE_DIM + MLA_V_DIM)
    wkv2 = jnp.concatenate([wkv[..., :MLA_NOPE_DIM].reshape(DEPTH, MLA_KV_RANK, -1),
                            wkv[..., MLA_NOPE_DIM:].reshape(DEPTH, MLA_KV_RANK, -1)], axis=-1).astype(BF16)
    wd_b = ffn_w_down.astype(BF16)
    pool_w_b = pool_w.astype(BF16)
    ws_s = cm_ws.at[:, :, CHUNK:, CHUNK:].set(cm_ws[:, :, :CHUNK, :CHUNK])
    ws2 = jnp.stack([cm_ws, ws_s], axis=1)
    bs_s = jnp.concatenate([cm_bs[:, :, :CHUNK], cm_bs[:, :, :CHUNK]], axis=-1)
    bs2 = jnp.swapaxes(jnp.stack([cm_bs, bs_s], axis=1), -1, -2)
    hist_p = jnp.zeros((1, bp, HIST_ROWS, D_POOL), F32)
    hist_s = jnp.pad(state_pool, ((0, 0), (0, 0), (HIST_ROWS - POOL_HIST, 0), (0, 0)))

    outs = {k: [] for k in ("pool_p", "pool_s")}
    ckv_all = kr_all = cmv_all = None
    h = prenorm(x[0], x[1], g_pre_mix[0])
    for l in range(DEPTH):
        last = l + 1 == DEPTH
        uv = matmul(h, w_in_t, l, n_off=0, n=OFF_POOL, act="gelu", wt=True, name="in_uv")
        rest = matmul(h, w_in_t, l, n_off=OFF_POOL, n=OFF_KR - OFF_POOL, wt=True, name="in_rest")
        kr, kr_all = kr_proj(h, w_in_t, l, cos_t, sin_t, kr_all, rows_first=mp)
        a, cmv_all = chunk_mlp(uv, cm_ln_g[l], cm_ln_b[l], ws2, bs2, l, mp, cmv_all)
        b = pool_mix(rest, hist_p, 0, pool_w_b, pool_scale[l], l, row0=0, n_seq=bp, seq_len=lp,
                     t_rows=512, pos0=0, m_total=m)
        b = pool_mix(rest, hist_s, l, pool_w_b, pool_scale[l], l, row0=mp, n_seq=bs_, seq_len=ls,
                     t_rows=ls, pos0=past, m_total=m, prev=b)
        qn, ckv_b, ckv_all = latnorm(rest, mla_q_norm[l], mla_kv_norm[l], l, ckv_all, rows_first=mp)
        q = matmul_rope(qn, wq2, l, cos_t, sin_t, head_w=Q_HEAD_PAD, out_dtype=BF16, scale=Q_SCALE, name="q_up")
        k_p, v_p = kv_up(ckv_b, kr, wkv2, l, mp)
        c = attention_prompt(q, k_p, v_p, n_batch=bp, seq_len=lp, m_total=m)
        c = attention_sample(q, cache_mla_ckv, cache_mla_krope, ckv_b, kr, wkv2, l, c,
                             row0=mp, n_batch=bs_, ls=ls)
        mix = mix_out(a, b, c, w_out, l)
        x, h2 = resnorm(x, mix, g_post_mix[l], g_pre_ffn[l])
        act = ffn_up(h2, ffn_w_gate, ffn_w_up, l)
        f = matmul(act, wd_b, l, out_dtype=BF16, tm=512, tn=512, name="ffn_down")
        x, h = resnorm(x, f, g_post_ffn[l], None if last else g_pre_mix[l + 1],
                       split_rows=mp, split_out=last)

        pool_raw = rest[:, :D_POOL]
        outs["pool_p"].append(pool_raw[:mp].reshape(bp, lp, D_POOL)[:, lp - POOL_HIST:])
        outs["pool_s"].append(pool_raw[mp:].reshape(bs_, ls, D_POOL)[:, ls - POOL_HIST:])

    return (x[0].reshape(bp, lp, d), x[1].reshape(bs_, ls, d),
            ckv_all[0].reshape(DEPTH, bp, lp, MLA_KV_RANK), kr_all[0].reshape(DEPTH, bp, lp, MLA_ROPE_DIM),
            jnp.stack(outs["pool_p"]),
            ckv_all[1].reshape(DEPTH, bs_, ls, MLA_KV_RANK), kr_all[1].reshape(DEPTH, bs_, ls, MLA_ROPE_DIM),
            jnp.stack(outs["pool_s"]), cmv_all.reshape(DEPTH, bs_, ls, D_CM))
```

```python
import functools

import jax
import jax.numpy as jnp
from jax import lax
from jax.experimental import pallas as pl
from jax.experimental.pallas import tpu as pltpu

D_MODEL = 4096
DEPTH = 4
CHUNK = 64
EPS = 1e-6
CM_CHUNK = 128
CM_HEADS = 8
CM_HEAD_DIM = 128
D_CM = CM_HEADS * CM_HEAD_DIM
POOL_WINDOWS = (2, 4, 8, 16)
POOL_GROUP_DIM = 256
D_POOL = len(POOL_WINDOWS) * POOL_GROUP_DIM
POOL_HIST = max(POOL_WINDOWS) - 1
HIST_ROWS = POOL_HIST + 1
MLA_HEADS = 16
MLA_Q_RANK = 1024
MLA_KV_RANK = 512
MLA_NOPE_DIM = 128
MLA_ROPE_DIM = 64
MLA_V_DIM = 128
D_MLA = MLA_HEADS * MLA_V_DIM
MLA_SCALE = (MLA_NOPE_DIM + MLA_ROPE_DIM) ** -0.5
Q_SCALE = MLA_SCALE * 1.4426950408889634
ROPE_THETA = 10000.0
Q_HEAD_PAD = 256
OFF_V = D_CM
OFF_POOL = 2 * D_CM
OFF_Q = OFF_POOL + D_POOL
OFF_CKV = OFF_Q + MLA_Q_RANK
OFF_KR = OFF_CKV + MLA_KV_RANK
FF_TILE = 256

LANES = 128
VMEM_LIMIT = 56 * 1024 * 1024

BF16 = jnp.bfloat16
F32 = jnp.float32


def _params(sem):
    return pltpu.CompilerParams(dimension_semantics=sem, vmem_limit_bytes=VMEM_LIMIT)


def _rms(x, g):
    r = lax.rsqrt(jnp.mean(x * x, axis=-1, keepdims=True) + EPS)
    return (x * r) * g


def _dot(a, b):
    return jnp.dot(a, b, preferred_element_type=F32)


def _dot_nt(a, b):
    return lax.dot_general(a, b, (((1,), (1,)), ((), ())), preferred_element_type=F32)


def _rope_lanes(b, c, s):
    return b * c + pltpu.roll(b, 64, axis=1) * s


def _two_source_specs(tm, d, n_first):
    return [pl.BlockSpec((tm, d), lambda i: (jnp.minimum(i, n_first - 1), 0)),
            pl.BlockSpec((tm, d), lambda i: (jnp.maximum(i - n_first, 0), 0))]


def _prenorm_kernel(xp_ref, xs_ref, g_ref, h_ref, *, n_first):
    x = jnp.where(pl.program_id(0) < n_first, xp_ref[...], xs_ref[...])
    h_ref[...] = _rms(x, g_ref[...]).astype(BF16)


def prenorm(xp, xs, g, tm=512):
    d = xp.shape[1]
    m = xp.shape[0] + xs.shape[0]
    n_first = xp.shape[0] // tm
    return pl.pallas_call(
        functools.partial(_prenorm_kernel, n_first=n_first),
        grid=(m // tm,),
        in_specs=_two_source_specs(tm, d, n_first) + [pl.BlockSpec((1, d), lambda i: (0, 0))],
        out_specs=pl.BlockSpec((tm, d), lambda i: (i, 0)),
        out_shape=jax.ShapeDtypeStruct((m, d), BF16),
        compiler_params=_params(("arbitrary",)),
        name="prenorm",
    )(xp, xs, g.reshape(1, d))


def _resnorm_kernel(*refs, n_first, split_in, split_out, with_h):
    refs = list(refs)
    if split_in:
        xp_ref, xs_ref = refs[:2]
        x = jnp.where(pl.program_id(0) < n_first, xp_ref[...], xs_ref[...])
        refs = refs[2:]
    else:
        x = refs[0][...]
        refs = refs[1:]
    f_ref, gpost_ref = refs[:2]
    refs = refs[2:]
    if with_h:
        gpre_ref = refs[0]
        refs = refs[1:]
    xn = x + _rms(f_ref[...].astype(F32), gpost_ref[...])
    if split_out:
        _store_split(xn, refs[0], refs[1], n_first)
        refs = refs[2:]
    else:
        refs[0][...] = xn
        refs = refs[1:]
    if with_h:
        refs[0][...] = _rms(xn, gpre_ref[...]).astype(BF16)


def resnorm(x, f, g_post, g_pre, *, split_rows=None, split_out=False):
    split_in = isinstance(x, (tuple, list))
    tm = 256 if split_in or split_out else 384
    m, d = f.shape
    with_h = g_pre is not None
    n_first = (x[0].shape[0] if split_in else split_rows if split_out else m) // tm
    row = pl.BlockSpec((tm, d), lambda i: (i, 0))
    vec = pl.BlockSpec((1, d), lambda i: (0, 0))
    in_specs = (_two_source_specs(tm, d, n_first) if split_in else [row]) + [row, vec]
    args = (list(x) if split_in else [x]) + [f, g_post.reshape(1, d)]
    if with_h:
        in_specs.append(vec)
        args.append(g_pre.reshape(1, d))
    if split_out:
        out_specs = _two_source_specs(tm, d, n_first)
        out_shape = [jax.ShapeDtypeStruct((split_rows, d), F32), jax.ShapeDtypeStruct((m - split_rows, d), F32)]
    else:
        out_specs = [row]
        out_shape = [jax.ShapeDtypeStruct((m, d), F32)]
    if with_h:
        out_specs.append(row)
        out_shape.append(jax.ShapeDtypeStruct((m, d), BF16))
    outs = pl.pallas_call(
        functools.partial(_resnorm_kernel, n_first=n_first, split_in=split_in, split_out=split_out, with_h=with_h),
        grid=(m // tm,), in_specs=in_specs, out_specs=out_specs, out_shape=out_shape,
        compiler_params=_params(("arbitrary",)), name="resnorm",
    )(*args)
    xo = tuple(outs[:2]) if split_out else outs[0]
    return xo, (outs[-1] if with_h else None)


def _latnorm_kernel(*refs, n_first, aliased):
    q_ref, c_ref, gq_ref, gc_ref = refs[:4]
    cp_ref, cs_ref, qn_ref, cb_ref = refs[6:] if aliased else refs[4:]
    qn_ref[...] = _rms(q_ref[...], gq_ref[...]).astype(BF16)
    cn = _rms(c_ref[...], gc_ref[...])
    cb_ref[...] = cn.astype(BF16)
    _store_split(cn, cp_ref, cs_ref, n_first)


def _store_split(val, first_ref, second_ref, n_first):
    @pl.when(pl.program_id(0) < n_first)
    def _():
        first_ref[...] = val

    @pl.when(pl.program_id(0) >= n_first)
    def _():
        second_ref[...] = val


def _layer_split_specs(tm, width, l, n_first, n_second):
    return [pl.BlockSpec((tm, width), lambda i, *_: (l * n_first + jnp.minimum(i, n_first - 1), 0)),
            pl.BlockSpec((tm, width), lambda i, *_: (l * n_second + jnp.maximum(i - n_first, 0), 0))]


def _layer_split_io(prev, rows_first, rows_second, width, n_in):
    shapes = [jax.ShapeDtypeStruct((DEPTH * rows_first, width), F32),
              jax.ShapeDtypeStruct((DEPTH * rows_second, width), F32)]
    if prev is None:
        return shapes, [], [], {}
    return shapes, [pl.BlockSpec(memory_space=pl.ANY)] * 2, list(prev), {n_in: 0, n_in + 1: 1}


def latnorm(rest, gq, gc, l, prev, *, rows_first, tm=512):
    m = rest.shape[0]
    qb = OFF_Q // MLA_Q_RANK
    cb = OFF_CKV // MLA_KV_RANK
    n_first, n_second = rows_first // tm, (m - rows_first) // tm
    shapes, extra_specs, extra_args, aliases = _layer_split_io(prev, rows_first, m - rows_first, MLA_KV_RANK, 4)
    outs = pl.pallas_call(
        functools.partial(_latnorm_kernel, n_first=n_first, aliased=prev is not None),
        grid=(m // tm,),
        in_specs=[pl.BlockSpec((tm, MLA_Q_RANK), lambda i: (i, qb)),
                  pl.BlockSpec((tm, MLA_KV_RANK), lambda i: (i, cb)),
                  pl.BlockSpec((1, MLA_Q_RANK), lambda i: (0, 0)),
                  pl.BlockSpec((1, MLA_KV_RANK), lambda i: (0, 0))] + extra_specs,
        out_specs=_layer_split_specs(tm, MLA_KV_RANK, l, n_first, n_second)
        + [pl.BlockSpec((tm, MLA_Q_RANK), lambda i: (i, 0)),
           pl.BlockSpec((tm, MLA_KV_RANK), lambda i: (i, 0))],
        out_shape=shapes + [jax.ShapeDtypeStruct((m, MLA_Q_RANK), BF16),
                            jax.ShapeDtypeStruct((m, MLA_KV_RANK), BF16)],
        input_output_aliases=aliases,
        compiler_params=_params(("arbitrary",)), name="latnorm",
    )(rest, rest, gq.reshape(1, -1), gc.reshape(1, -1), *extra_args)
    return outs[2], outs[3], (outs[0], outs[1])


def _mm_kernel(a_ref, w_ref, o_ref, *, gelu_tiles, wt):
    r = (_dot_nt if wt else _dot)(a_ref[...], w_ref[...].astype(BF16))
    if gelu_tiles:
        @pl.when(pl.program_id(1) < gelu_tiles)
        def _():
            o_ref[...] = jax.nn.gelu(r).astype(o_ref.dtype)

        @pl.when(pl.program_id(1) >= gelu_tiles)
        def _():
            o_ref[...] = r.astype(o_ref.dtype)
    else:
        o_ref[...] = r.astype(o_ref.dtype)


def matmul(a, w, l, *, n=None, gelu_cols=0, wt=False, out_dtype=F32, tm=1024, tn=512, name="mm"):
    m, k = a.shape
    n = w.shape[1 if wt else 2] if n is None else n
    wspec = (pl.BlockSpec((None, tn, k), lambda i, j: (l, j, 0)) if wt
             else pl.BlockSpec((None, k, tn), lambda i, j: (l, 0, j)))
    return pl.pallas_call(
        functools.partial(_mm_kernel, gelu_tiles=gelu_cols // tn, wt=wt),
        grid=(m // tm, n // tn),
        in_specs=[pl.BlockSpec((tm, k), lambda i, j: (i, 0)), wspec],
        out_specs=pl.BlockSpec((tm, tn), lambda i, j: (i, j)),
        out_shape=jax.ShapeDtypeStruct((m, n), out_dtype),
        compiler_params=_params(("parallel", "arbitrary")), name=name,
    )(a, w)


def _ffn_up_kernel(h_ref, wg_ref, wu_ref, o_ref):
    h = h_ref[...]
    g = _dot(h, wg_ref[...].astype(BF16))
    u = _dot(h, wu_ref[...].astype(BF16))
    o_ref[...] = (jax.nn.silu(g) * u).astype(o_ref.dtype)


def ffn_up(h, wg, wu, l, tm=1536, tn=FF_TILE):
    m, k = h.shape
    n = wg.shape[2]
    wspec = pl.BlockSpec((None, k, tn), lambda i, j: (l, 0, j))
    return pl.pallas_call(
        _ffn_up_kernel,
        grid=(m // tm, n // tn),
        in_specs=[pl.BlockSpec((tm, k), lambda i, j: (i, 0)), wspec, wspec],
        out_specs=pl.BlockSpec((tm, tn), lambda i, j: (i, j)),
        out_shape=jax.ShapeDtypeStruct((m, n), BF16),
        compiler_params=_params(("parallel", "arbitrary")), name="ffn_up",
    )(h, wg, wu)


def _q_up_kernel(a_ref, w_ref, c_ref, s_ref, o_ref, *, scale):
    r = _dot(a_ref[...], w_ref[...]) * scale
    c = c_ref[...]
    s = s_ref[...]
    for h in range(r.shape[1] // Q_HEAD_PAD):
        lo = h * Q_HEAD_PAD
        o_ref[:, lo:lo + MLA_NOPE_DIM] = r[:, lo:lo + MLA_NOPE_DIM].astype(o_ref.dtype)
        b = r[:, lo + MLA_NOPE_DIM:lo + Q_HEAD_PAD]
        o_ref[:, lo + MLA_NOPE_DIM:lo + Q_HEAD_PAD] = _rope_lanes(b, c, s).astype(o_ref.dtype)


def q_up(a, w, l, cos_t, sin_t, *, scale, tm=1024, tn=4 * Q_HEAD_PAD):
    m, k = a.shape
    n = w.shape[2]
    return pl.pallas_call(
        functools.partial(_q_up_kernel, scale=scale),
        grid=(m // tm, n // tn),
        in_specs=[pl.BlockSpec((tm, k), lambda i, j: (i, 0)),
                  pl.BlockSpec((None, k, tn), lambda i, j: (l, 0, j)),
                  pl.BlockSpec((tm, LANES), lambda i, j: (i, 0)),
                  pl.BlockSpec((tm, LANES), lambda i, j: (i, 0))],
        out_specs=pl.BlockSpec((tm, tn), lambda i, j: (i, j)),
        out_shape=jax.ShapeDtypeStruct((m, n), BF16),
        compiler_params=_params(("parallel", "arbitrary")), name="q_up",
    )(a, w, cos_t, sin_t)


def _kr_kernel(*refs, n_first, aliased):
    a_ref, w_ref, c_ref, s_ref = refs[:4]
    kp_ref, ks_ref, o_ref = refs[6:] if aliased else refs[4:]
    r = _dot_nt(a_ref[...], w_ref[...].astype(BF16))
    lane = lax.broadcasted_iota(jnp.int32, r.shape, 1)
    half = MLA_ROPE_DIM // 2
    partner = jnp.where(lane < half, pltpu.roll(r, LANES - half, axis=1), pltpu.roll(r, half, axis=1))
    kr = jnp.where(lane < MLA_ROPE_DIM, r * c_ref[...] + partner * s_ref[...], 0.0)
    o_ref[...] = kr
    _store_split(kr[:, :MLA_ROPE_DIM], kp_ref, ks_ref, n_first)


def kr_proj(h, w_in_t, l, cos_t, sin_t, prev, *, rows_first, tm=1024):
    m, k = h.shape
    n_first, n_second = rows_first // tm, (m - rows_first) // tm
    shapes, extra_specs, extra_args, aliases = _layer_split_io(prev, rows_first, m - rows_first, MLA_ROPE_DIM, 4)
    outs = pl.pallas_call(
        functools.partial(_kr_kernel, n_first=n_first, aliased=prev is not None),
        grid=(m // tm,),
        in_specs=[pl.BlockSpec((tm, k), lambda i: (i, 0)),
                  pl.BlockSpec((None, LANES, k), lambda i: (l, OFF_KR // LANES, 0)),
                  pl.BlockSpec((tm, LANES), lambda i: (i, 0)),
                  pl.BlockSpec((tm, LANES), lambda i: (i, 0))] + extra_specs,
        out_specs=_layer_split_specs(tm, MLA_ROPE_DIM, l, n_first, n_second)
        + [pl.BlockSpec((tm, LANES), lambda i: (i, 0))],
        out_shape=shapes + [jax.ShapeDtypeStruct((m, LANES), F32)],
        input_output_aliases=aliases,
        compiler_params=_params(("arbitrary",)), name="in_kr",
    )(h, w_in_t, cos_t, sin_t, *extra_args)
    return outs[2], (outs[0], outs[1])


def _mix_out_kernel(a_ref, b_ref, c_ref, wa_ref, wb_ref, wc_ref, o_ref):
    o_ref[...] = (_dot(a_ref[...], wa_ref[...].astype(BF16)) + _dot(b_ref[...], wb_ref[...].astype(BF16))
                  + _dot(c_ref[...], wc_ref[...].astype(BF16))).astype(o_ref.dtype)


def mix_out(a, b, c, w_out, l, tm=1024, tn=512):
    m = a.shape[0]
    n = w_out.shape[2]
    ka, kb, kc = a.shape[1], b.shape[1], c.shape[1]
    return pl.pallas_call(
        _mix_out_kernel,
        grid=(m // tm, n // tn),
        in_specs=[pl.BlockSpec((tm, ka), lambda i, j: (i, 0)),
                  pl.BlockSpec((tm, kb), lambda i, j: (i, 0)),
                  pl.BlockSpec((tm, kc), lambda i, j: (i, 0)),
                  pl.BlockSpec((None, ka, tn), lambda i, j: (l, 0, j)),
                  pl.BlockSpec((None, kb, tn), lambda i, j: (l, ka // kb, j)),
                  pl.BlockSpec((None, kc, tn), lambda i, j: (l, (ka + kb) // kc, j))],
        out_specs=pl.BlockSpec((tm, tn), lambda i, j: (i, j)),
        out_shape=jax.ShapeDtypeStruct((m, n), BF16),
        compiler_params=_params(("parallel", "arbitrary")), name="mix_out",
    )(a, b, c, w_out, w_out, w_out)


def _chunk_mlp_kernel(*refs, first_sample_block, aliased):
    u_ref, v_ref, g_ref, b_ref, w_ref, bs_ref = refs[:6]
    a_ref, vn_ref = refs[7:] if aliased else refs[6:]
    is_sample = pl.program_id(0) >= first_sample_block
    ri = lax.broadcasted_iota(jnp.int32, (CM_CHUNK, CM_CHUNK), 0) // CHUNK
    ci = lax.broadcasted_iota(jnp.int32, (CM_CHUNK, CM_CHUNK), 1) // CHUNK
    max_lag = jnp.where(is_sample, 0, 1)
    keep = (ri - ci >= 0) & (ri - ci <= max_lag)
    bs = bs_ref[0]
    ws = [jnp.where(keep, w_ref[0, h], 0.0).astype(BF16) for h in range(CM_HEADS)]
    for c in range(u_ref.shape[0] // CM_CHUNK):
        rows = slice(c * CM_CHUNK, (c + 1) * CM_CHUNK)
        v = v_ref[rows, :]
        mu = jnp.mean(v, axis=-1, keepdims=True)
        d = v - mu
        var = jnp.mean(d * d, axis=-1, keepdims=True)
        vn = (d * lax.rsqrt(var + EPS)) * g_ref[...] + b_ref[...]
        @pl.when(is_sample)
        def _():
            vn_ref[rows, :] = vn

        vb = vn.astype(BF16)
        for h in range(CM_HEADS):
            cols = slice(h * CM_HEAD_DIM, (h + 1) * CM_HEAD_DIM)
            sv = _dot(ws[h], vb[:, cols]) + bs[:, h:h + 1]
            a_ref[rows, cols] = (u_ref[rows, cols] * sv).astype(a_ref.dtype)


CM_ROWS = 4 * CM_CHUNK


def chunk_mlp(uv, ln_g, ln_b, ws2, bs2, l, first_sample_row, prev):
    m = uv.shape[0]
    first_sample_block = first_sample_row // CM_ROWS
    n_sample = m // CM_ROWS - first_sample_block
    sel = lambda c: jnp.where(c >= first_sample_block, 1, 0)
    extra_specs, extra_args, aliases = [], [], {}
    if prev is not None:
        extra_specs, extra_args, aliases = [pl.BlockSpec(memory_space=pl.ANY)], [prev], {6: 1}
    return pl.pallas_call(
        functools.partial(_chunk_mlp_kernel, first_sample_block=first_sample_block, aliased=prev is not None),
        grid=(m // CM_ROWS,),
        in_specs=[pl.BlockSpec((CM_ROWS, D_CM), lambda c: (c, 0)),
                  pl.BlockSpec((CM_ROWS, D_CM), lambda c: (c, 1)),
                  pl.BlockSpec((1, D_CM), lambda c: (0, 0)),
                  pl.BlockSpec((1, D_CM), lambda c: (0, 0)),
                  pl.BlockSpec((None, 1, CM_HEADS, CM_CHUNK, CM_CHUNK), lambda c: (l, sel(c), 0, 0, 0)),
                  pl.BlockSpec((None, 1, CM_CHUNK, CM_HEADS), lambda c: (l, sel(c), 0, 0))] + extra_specs,
        out_specs=[pl.BlockSpec((CM_ROWS, D_CM), lambda c: (c, 0)),
                   pl.BlockSpec((CM_ROWS, D_CM),
                                lambda c: (l * n_sample + jnp.maximum(c - first_sample_block, 0), 0))],
        out_shape=[jax.ShapeDtypeStruct((m, D_CM), BF16),
                   jax.ShapeDtypeStruct((DEPTH * n_sample * CM_ROWS, D_CM), F32)],
        input_output_aliases=aliases,
        compiler_params=_params(("arbitrary",)), name="chunk_mlp",
    )(uv, uv, ln_g.reshape(1, -1), ln_b.reshape(1, -1), ws2, bs2, *extra_args)


def _pool_kernel(*refs, t_rows, pos0, aliased):
    if aliased:
        x_ref, hist_ref, w_ref, sc_ref, _, o_ref, e_ref = refs
    else:
        x_ref, hist_ref, w_ref, sc_ref, o_ref, e_ref = refs
    t = pl.program_id(1)

    @pl.when(t == 0)
    def _():
        e_ref[0:HIST_ROWS, :] = hist_ref[0]

    x = x_ref[...]
    e_ref[HIST_ROWS:HIST_ROWS + t_rows, :] = x
    pos = pos0 + t * t_rows + lax.broadcasted_iota(jnp.int32, (t_rows, 1), 0)
    for g, w in enumerate(POOL_WINDOWS):
        cols = slice(g * POOL_GROUP_DIM, (g + 1) * POOL_GROUP_DIM)
        acc = x[:, cols]
        for k in range(1, w):
            acc = acc + e_ref[HIST_ROWS - k:HIST_ROWS - k + t_rows, cols]
        cnt = jnp.minimum(w, pos + 1).astype(F32)
        mg = (acc / cnt - x[:, cols]).astype(BF16)
        y = _dot(mg, w_ref[g]) * sc_ref[:, cols]
        o_ref[:, cols] = y.astype(o_ref.dtype)
    e_ref[0:HIST_ROWS, :] = e_ref[t_rows:t_rows + HIST_ROWS, :]


def pool_mix(rest, hist, hl, pool_w, pool_scale, l, *, row0, n_seq, seq_len, t_rows, pos0, m_total, prev=None):
    n_t = seq_len // t_rows
    b0 = row0 // t_rows
    row_map = lambda s, t: (b0 + s * n_t + t, 0)
    in_specs = [pl.BlockSpec((t_rows, D_POOL), lambda s, t: (b0 + s * n_t + t, OFF_POOL // D_POOL)),
                pl.BlockSpec((None, 1, HIST_ROWS, D_POOL), lambda s, t: (hl, s, 0, 0)),
                pl.BlockSpec((None,) + pool_w.shape[1:], lambda s, t: (l, 0, 0, 0)),
                pl.BlockSpec((1, D_POOL), lambda s, t: (0, 0))]
    args = [rest, hist, pool_w, pool_scale.reshape(1, -1)]
    aliases = {}
    if prev is not None:
        in_specs.append(pl.BlockSpec(memory_space=pl.ANY))
        args.append(prev)
        aliases = {4: 0}
    return pl.pallas_call(
        functools.partial(_pool_kernel, t_rows=t_rows, pos0=pos0, aliased=prev is not None),
        grid=(n_seq, n_t),
        in_specs=in_specs,
        out_specs=pl.BlockSpec((t_rows, D_POOL), row_map),
        out_shape=jax.ShapeDtypeStruct((m_total, D_POOL), BF16),
        scratch_shapes=[pltpu.VMEM((HIST_ROWS + t_rows, D_POOL), F32)],
        input_output_aliases=aliases,
        compiler_params=_params(("parallel", "arbitrary")), name="pool_mix",
    )(*args)


KV_HEADS_PER_TILE = 8


def _kv_up_k_kernel(c_ref, w_ref, kr_ref, o_ref):
    r = _dot(c_ref[...], w_ref[...])
    kr = kr_ref[...].astype(BF16)
    for hh in range(KV_HEADS_PER_TILE):
        o_ref[hh, :, 0:MLA_NOPE_DIM] = r[:, hh * MLA_NOPE_DIM:(hh + 1) * MLA_NOPE_DIM].astype(BF16)
        o_ref[hh, :, MLA_NOPE_DIM:Q_HEAD_PAD] = kr


def _kv_up_v_kernel(c_ref, w_ref, o_ref):
    r = _dot(c_ref[...], w_ref[...])
    for hh in range(KV_HEADS_PER_TILE):
        o_ref[hh] = r[:, hh * MLA_V_DIM:(hh + 1) * MLA_V_DIM].astype(BF16)


def kv_up(ckv_b, kr, wkv2, l, rows, tm=1024):
    kd = ckv_b.shape[1]
    tn = KV_HEADS_PER_TILE * MLA_NOPE_DIM
    nj = MLA_HEADS // KV_HEADS_PER_TILE
    k = pl.pallas_call(
        _kv_up_k_kernel,
        grid=(rows // tm, nj),
        in_specs=[pl.BlockSpec((tm, kd), lambda i, j: (i, 0)),
                  pl.BlockSpec((None, kd, tn), lambda i, j: (l, 0, j)),
                  pl.BlockSpec((tm, LANES), lambda i, j: (i, 0))],
        out_specs=pl.BlockSpec((KV_HEADS_PER_TILE, tm, Q_HEAD_PAD), lambda i, j: (j, i, 0)),
        out_shape=jax.ShapeDtypeStruct((MLA_HEADS, rows, Q_HEAD_PAD), BF16),
        compiler_params=_params(("parallel", "arbitrary")), name="kv_up_k",
    )(ckv_b, wkv2, kr)
    v = pl.pallas_call(
        _kv_up_v_kernel,
        grid=(rows // tm, nj),
        in_specs=[pl.BlockSpec((tm, kd), lambda i, j: (i, 0)),
                  pl.BlockSpec((None, kd, tn), lambda i, j: (l, 0, j + nj))],
        out_specs=pl.BlockSpec((KV_HEADS_PER_TILE, tm, MLA_V_DIM), lambda i, j: (j, i, 0)),
        out_shape=jax.ShapeDtypeStruct((MLA_HEADS, rows, MLA_V_DIM), BF16),
        compiler_params=_params(("parallel", "arbitrary")), name="kv_up_v",
    )(ckv_b, wkv2)
    return k, v


def _attn_prompt_kernel(q_ref, k_ref, v_ref, o_ref, qh_ref, m_ref, l_ref, acc_ref, *, t):
    qi = pl.program_id(1)
    ki = pl.program_id(2)

    @pl.when(ki == 0)
    def _():
        for h in range(MLA_HEADS):
            qh_ref[h] = q_ref[:, h * Q_HEAD_PAD:(h + 1) * Q_HEAD_PAD]
        m_ref[...] = jnp.full_like(m_ref, -jnp.inf)
        l_ref[...] = jnp.zeros_like(l_ref)
        acc_ref[...] = jnp.zeros_like(acc_ref)

    def all_heads(bias):
        def head(h, carry):
            s = _dot_nt(qh_ref[h], k_ref[h])
            if bias is not None:
                s = s + bias
            m_prev = m_ref[h]
            m_new = jnp.maximum(m_prev, jnp.max(s, axis=-1, keepdims=True))
            alpha = jnp.exp2(m_prev - m_new)
            p = jnp.exp2(s - jnp.tile(m_new, (1, t // LANES)))
            l_ref[h] = alpha * l_ref[h] + jnp.sum(p, axis=-1, keepdims=True)
            acc_ref[h] = alpha * acc_ref[h] + _dot(p.astype(BF16), v_ref[h])
            m_ref[h] = m_new
            return carry
        lax.fori_loop(0, MLA_HEADS, head, 0, unroll=4)

    @pl.when(ki < qi)
    def _():
        all_heads(None)

    @pl.when(ki == qi)
    def _():
        rc = lax.broadcasted_iota(jnp.int32, (t, t), 0) // CHUNK
        cc = lax.broadcasted_iota(jnp.int32, (t, t), 1) // CHUNK
        all_heads(jnp.where(cc <= rc, 0.0, -jnp.inf))

    @pl.when(ki == pl.num_programs(2) - 1)
    def _():
        for h in range(MLA_HEADS):
            o_ref[:, h * MLA_V_DIM:(h + 1) * MLA_V_DIM] = (acc_ref[h] / l_ref[h]).astype(o_ref.dtype)


def attention_prompt(q, k, v, *, n_batch, seq_len, m_total, t=512):
    nt = seq_len // t
    kmap = lambda b, qi, ki: (0, b * nt + jnp.minimum(ki, qi), 0)
    return pl.pallas_call(
        functools.partial(_attn_prompt_kernel, t=t),
        grid=(n_batch, nt, nt),
        in_specs=[pl.BlockSpec((t, MLA_HEADS * Q_HEAD_PAD), lambda b, qi, ki: (b * nt + qi, 0)),
                  pl.BlockSpec((MLA_HEADS, t, Q_HEAD_PAD), kmap),
                  pl.BlockSpec((MLA_HEADS, t, MLA_V_DIM), kmap)],
        out_specs=pl.BlockSpec((t, D_MLA), lambda b, qi, ki: (b * nt + qi, 0)),
        out_shape=jax.ShapeDtypeStruct((m_total, D_MLA), BF16),
        scratch_shapes=[pltpu.VMEM((MLA_HEADS, t, Q_HEAD_PAD), BF16),
                        pltpu.VMEM((MLA_HEADS, t, LANES), F32),
                        pltpu.VMEM((MLA_HEADS, t, LANES), F32),
                        pltpu.VMEM((MLA_HEADS, t, MLA_V_DIM), F32)],
        compiler_params=_params(("parallel", "parallel", "arbitrary")), name="attn_prompt",
    )(q, k, v)


def _attn_sample_kernel(q_ref, cp_ref, kp_ref, cn_ref, kn_ref, wk_ref, wv_ref, _, o_ref,
                        qa_ref, qr_ref, ol_ref, *, ls):
    q = q_ref[...]
    for h in range(MLA_HEADS):
        rows = slice(h * ls, (h + 1) * ls)
        lo = h * Q_HEAD_PAD
        wk_h = wk_ref[:, h * MLA_NOPE_DIM:(h + 1) * MLA_NOPE_DIM]
        qa_ref[rows, :] = _dot_nt(q[:, lo:lo + MLA_NOPE_DIM], wk_h).astype(BF16)
        qr_ref[rows, :] = q[:, lo + MLA_NOPE_DIM:lo + Q_HEAD_PAD]
    cp = cp_ref[...].astype(BF16)
    kp = kp_ref[...].astype(BF16)
    cn = cn_ref[...]
    kn = kn_ref[...].astype(BF16)
    qa = qa_ref[...]
    qr = qr_ref[...]
    s_p = _dot_nt(qa, cp) + _dot_nt(qr[:, :MLA_ROPE_DIM], kp)
    s_n = _dot_nt(qa, cn) + _dot_nt(qr, kn)
    m = jnp.maximum(jnp.max(s_p, axis=-1, keepdims=True), jnp.max(s_n, axis=-1, keepdims=True))
    p_p = jnp.exp2(s_p - m)
    p_n = jnp.exp2(s_n - m)
    den = jnp.sum(p_p, axis=-1, keepdims=True) + jnp.sum(p_n, axis=-1, keepdims=True)
    ol = (_dot(p_p.astype(BF16), cp) + _dot(p_n.astype(BF16), cn)) / den
    ol_ref[...] = ol.astype(BF16)
    for h in range(MLA_HEADS):
        wv_h = wv_ref[:, h * MLA_V_DIM:(h + 1) * MLA_V_DIM]
        o_ref[:, h * MLA_V_DIM:(h + 1) * MLA_V_DIM] = _dot(ol_ref[h * ls:(h + 1) * ls, :], wv_h).astype(o_ref.dtype)


def attention_sample(q, cache_ckv, cache_kr, ckv_b, kr, wkv2, l, prev, *, row0, n_batch, ls):
    past = cache_ckv.shape[2]
    rb0 = row0 // ls
    rmap = lambda b: (rb0 + b, 0)
    hk = MLA_HEADS * MLA_NOPE_DIM
    return pl.pallas_call(
        functools.partial(_attn_sample_kernel, ls=ls),
        grid=(n_batch,),
        in_specs=[pl.BlockSpec((ls, MLA_HEADS * Q_HEAD_PAD), rmap),
                  pl.BlockSpec((None, None, past, MLA_KV_RANK), lambda b: (l, b, 0, 0)),
                  pl.BlockSpec((None, None, past, MLA_ROPE_DIM), lambda b: (l, b, 0, 0)),
                  pl.BlockSpec((ls, MLA_KV_RANK), rmap),
                  pl.BlockSpec((ls, LANES), rmap),
                  pl.BlockSpec((None, MLA_KV_RANK, hk), lambda b: (l, 0, 0)),
                  pl.BlockSpec((None, MLA_KV_RANK, hk), lambda b: (l, 0, 1)),
                  pl.BlockSpec(memory_space=pl.ANY)],
        out_specs=pl.BlockSpec((ls, D_MLA), rmap),
        out_shape=jax.ShapeDtypeStruct(prev.shape, prev.dtype),
        scratch_shapes=[pltpu.VMEM((MLA_HEADS * ls, MLA_KV_RANK), BF16),
                        pltpu.VMEM((MLA_HEADS * ls, LANES), BF16),
                        pltpu.VMEM((MLA_HEADS * ls, MLA_KV_RANK), BF16)],
        input_output_aliases={7: 0},
        compiler_params=_params(("parallel",)), name="attn_sample",
    )(q, cache_ckv, cache_kr, ckv_b, kr, wkv2, wkv2, prev)


def _rope_tables(pos):
    half = MLA_ROPE_DIM // 2
    inv = ROPE_THETA ** (-jnp.arange(half, dtype=F32) / half)
    ang = pos.astype(F32)[:, None] * inv[None, :]
    cos, sin = jnp.cos(ang), jnp.sin(ang)
    z = jnp.zeros((pos.shape[0], LANES - MLA_ROPE_DIM), F32)
    return jnp.concatenate([cos, cos, z], axis=1), jnp.concatenate([-sin, sin, z], axis=1)


def _swap_halves(w):
    half = w.shape[-1] // 2
    return jnp.concatenate([w[..., half:], w[..., :half]], axis=-1)


def kernel(x_prompt, x_sample, cache_mla_ckv, cache_mla_krope, state_pool, g_pre_mix, g_post_mix, g_pre_ffn, g_post_ffn, w_in, cm_ln_g, cm_ln_b, cm_ws, cm_bs, pool_w, pool_scale, mla_q_norm, mla_w_uq, mla_kv_norm, mla_w_ukv, w_out, ffn_w_gate, ffn_w_up, ffn_w_down):
    bp, lp, d = x_prompt.shape
    bs_, ls, _ = x_sample.shape
    past = cache_mla_ckv.shape[2]
    mp, ms = bp * lp, bs_ * ls
    m = mp + ms

    x = (x_prompt.reshape(mp, d), x_sample.reshape(ms, d))
    pos = jnp.concatenate([jnp.tile(jnp.arange(lp), bp), jnp.tile(past + jnp.arange(ls), bs_)])
    cos_t, sin_t = _rope_tables(pos)

    w_in_t = jnp.swapaxes(w_in, 1, 2)
    wq = mla_w_uq.reshape(DEPTH, MLA_Q_RANK, MLA_HEADS, MLA_NOPE_DIM + MLA_ROPE_DIM)
    wq_rope = wq[..., MLA_NOPE_DIM:]
    wq2 = jnp.concatenate([wq[..., :MLA_NOPE_DIM], wq_rope, _swap_halves(wq_rope)], axis=-1)
    wq2 = wq2.reshape(DEPTH, MLA_Q_RANK, MLA_HEADS * Q_HEAD_PAD).astype(BF16)
    wkv = mla_w_ukv.reshape(DEPTH, MLA_KV_RANK, MLA_HEADS, MLA_NOPE_DIM + MLA_V_DIM)
    wkv2 = jnp.concatenate([wkv[..., :MLA_NOPE_DIM].reshape(DEPTH, MLA_KV_RANK, -1),
                            wkv[..., MLA_NOPE_DIM:].reshape(DEPTH, MLA_KV_RANK, -1)], axis=-1).astype(BF16)
    wd_b = ffn_w_down.astype(BF16)
    pool_w_b = pool_w.astype(BF16)
    ws_s = cm_ws.at[:, :, CHUNK:, CHUNK:].set(cm_ws[:, :, :CHUNK, :CHUNK])
    ws2 = jnp.stack([cm_ws, ws_s], axis=1)
    bs_s = jnp.concatenate([cm_bs[:, :, :CHUNK], cm_bs[:, :, :CHUNK]], axis=-1)
    bs2 = jnp.swapaxes(jnp.stack([cm_bs, bs_s], axis=1), -1, -2)
    hist_p = jnp.zeros((1, bp, HIST_ROWS, D_POOL), F32)
    hist_s = jnp.pad(state_pool, ((0, 0), (0, 0), (HIST_ROWS - POOL_HIST, 0), (0, 0)))

    outs = {k: [] for k in ("pool_p", "pool_s")}
    ckv_all = kr_all = cmv_all = None
    h = prenorm(x[0], x[1], g_pre_mix[0])
    for l in range(DEPTH):
        last = l + 1 == DEPTH
        rest = matmul(h, w_in_t, l, n=OFF_KR, gelu_cols=OFF_POOL, wt=True, name="in_proj")
        kr, kr_all = kr_proj(h, w_in_t, l, cos_t, sin_t, kr_all, rows_first=mp)
        a, cmv_all = chunk_mlp(rest, cm_ln_g[l], cm_ln_b[l], ws2, bs2, l, mp, cmv_all)
        b = pool_mix(rest, hist_p, 0, pool_w_b, pool_scale[l], l, row0=0, n_seq=bp, seq_len=lp,
                     t_rows=512, pos0=0, m_total=m)
        b = pool_mix(rest, hist_s, l, pool_w_b, pool_scale[l], l, row0=mp, n_seq=bs_, seq_len=ls,
                     t_rows=ls, pos0=past, m_total=m, prev=b)
        qn, ckv_b, ckv_all = latnorm(rest, mla_q_norm[l], mla_kv_norm[l], l, ckv_all, rows_first=mp)
        q = q_up(qn, wq2, l, cos_t, sin_t, scale=Q_SCALE)
        k_p, v_p = kv_up(ckv_b, kr, wkv2, l, mp)
        c = attention_prompt(q, k_p, v_p, n_batch=bp, seq_len=lp, m_total=m)
        c = attention_sample(q, cache_mla_ckv, cache_mla_krope, ckv_b, kr, wkv2, l, c,
                             row0=mp, n_batch=bs_, ls=ls)
        mix = mix_out(a, b, c, w_out, l)
        x, h2 = resnorm(x, mix, g_post_mix[l], g_pre_ffn[l])
        act = ffn_up(h2, ffn_w_gate, ffn_w_up, l)
        f = matmul(act, wd_b, l, out_dtype=BF16, tm=512, name="ffn_down")
        x, h = resnorm(x, f, g_post_ffn[l], None if last else g_pre_mix[l + 1],
                       split_rows=mp, split_out=last)

        tail = rest.reshape(m // ls, ls, rest.shape[1])[:, ls - POOL_HIST:, OFF_POOL:OFF_POOL + D_POOL]
        outs["pool_p"].append(tail[lp // ls - 1:mp // ls:lp // ls])
        outs["pool_s"].append(tail[mp // ls:])

    return (x[0].reshape(bp, lp, d), x[1].reshape(bs_, ls, d),
            ckv_all[0].reshape(DEPTH, bp, lp, MLA_KV_RANK), kr_all[0].reshape(DEPTH, bp, lp, MLA_ROPE_DIM),
            jnp.stack(outs["pool_p"]),
            ckv_all[1].reshape(DEPTH, bs_, ls, MLA_KV_RANK), kr_all[1].reshape(DEPTH, bs_, ls, MLA_ROPE_DIM),
            jnp.stack(outs["pool_s"]), cmv_all.reshape(DEPTH, bs_, ls, D_CM))
```

```python
import functools

import jax
import jax.numpy as jnp
from jax import lax
from jax.experimental import pallas as pl
from jax.experimental.pallas import tpu as pltpu

D_MODEL = 4096
DEPTH = 4
CHUNK = 64
EPS = 1e-6
CM_CHUNK = 128
CM_HEADS = 8
CM_HEAD_DIM = 128
D_CM = CM_HEADS * CM_HEAD_DIM
POOL_WINDOWS = (2, 4, 8, 16)
POOL_GROUP_DIM = 256
D_POOL = len(POOL_WINDOWS) * POOL_GROUP_DIM
POOL_HIST = max(POOL_WINDOWS) - 1
HIST_ROWS = POOL_HIST + 1
MLA_HEADS = 16
MLA_Q_RANK = 1024
MLA_KV_RANK = 512
MLA_NOPE_DIM = 128
MLA_ROPE_DIM = 64
MLA_V_DIM = 128
D_MLA = MLA_HEADS * MLA_V_DIM
MLA_SCALE = (MLA_NOPE_DIM + MLA_ROPE_DIM) ** -0.5
Q_SCALE = MLA_SCALE * 1.4426950408889634
ROPE_THETA = 10000.0
Q_HEAD_PAD = 256
OFF_V = D_CM
OFF_POOL = 2 * D_CM
OFF_Q = OFF_POOL + D_POOL
OFF_CKV = OFF_Q + MLA_Q_RANK
OFF_KR = OFF_CKV + MLA_KV_RANK
FF_TILE = 256

LANES = 128
VMEM_LIMIT = 56 * 1024 * 1024

BF16 = jnp.bfloat16
F32 = jnp.float32


def _params(sem):
    return pltpu.CompilerParams(dimension_semantics=sem, vmem_limit_bytes=VMEM_LIMIT)


def _rms(x, g):
    r = lax.rsqrt(jnp.mean(x * x, axis=-1, keepdims=True) + EPS)
    return (x * r) * g


def _dot(a, b):
    return jnp.dot(a, b, preferred_element_type=F32)


def _dot_nt(a, b):
    return lax.dot_general(a, b, (((1,), (1,)), ((), ())), preferred_element_type=F32)


def _rope_lanes(b, c, s):
    return b * c + pltpu.roll(b, 64, axis=1) * s


def _two_source_specs(tm, d, n_first):
    return [pl.BlockSpec((tm, d), lambda i: (jnp.minimum(i, n_first - 1), 0)),
            pl.BlockSpec((tm, d), lambda i: (jnp.maximum(i - n_first, 0), 0))]


def _prenorm_kernel(xp_ref, xs_ref, g_ref, h_ref, *, n_first):
    x = jnp.where(pl.program_id(0) < n_first, xp_ref[...], xs_ref[...])
    h_ref[...] = _rms(x, g_ref[...]).astype(BF16)


def prenorm(xp, xs, g, tm=512):
    d = xp.shape[1]
    m = xp.shape[0] + xs.shape[0]
    n_first = xp.shape[0] // tm
    return pl.pallas_call(
        functools.partial(_prenorm_kernel, n_first=n_first),
        grid=(m // tm,),
        in_specs=_two_source_specs(tm, d, n_first) + [pl.BlockSpec((1, d), lambda i: (0, 0))],
        out_specs=pl.BlockSpec((tm, d), lambda i: (i, 0)),
        out_shape=jax.ShapeDtypeStruct((m, d), BF16),
        compiler_params=_params(("arbitrary",)),
        name="prenorm",
    )(xp, xs, g.reshape(1, d))


def _resnorm_kernel(*refs, n_first, split_in, split_out, with_h):
    refs = list(refs)
    if split_in:
        xp_ref, xs_ref = refs[:2]
        x = jnp.where(pl.program_id(0) < n_first, xp_ref[...], xs_ref[...])
        refs = refs[2:]
    else:
        x = refs[0][...]
        refs = refs[1:]
    f_ref, gpost_ref = refs[:2]
    refs = refs[2:]
    if with_h:
        gpre_ref = refs[0]
        refs = refs[1:]
    xn = x + _rms(f_ref[...].astype(F32), gpost_ref[...])
    if split_out:
        _store_split(xn, refs[0], refs[1], n_first)
        refs = refs[2:]
    else:
        refs[0][...] = xn
        refs = refs[1:]
    if with_h:
        refs[0][...] = _rms(xn, gpre_ref[...]).astype(BF16)


def resnorm(x, f, g_post, g_pre, *, split_rows=None, split_out=False):
    split_in = isinstance(x, (tuple, list))
    tm = 256 if split_in or split_out else 384
    m, d = f.shape
    with_h = g_pre is not None
    n_first = (x[0].shape[0] if split_in else split_rows if split_out else m) // tm
    row = pl.BlockSpec((tm, d), lambda i: (i, 0))
    vec = pl.BlockSpec((1, d), lambda i: (0, 0))
    in_specs = (_two_source_specs(tm, d, n_first) if split_in else [row]) + [row, vec]
    args = (list(x) if split_in else [x]) + [f, g_post.reshape(1, d)]
    if with_h:
        in_specs.append(vec)
        args.append(g_pre.reshape(1, d))
    if split_out:
        out_specs = _two_source_specs(tm, d, n_first)
        out_shape = [jax.ShapeDtypeStruct((split_rows, d), F32), jax.ShapeDtypeStruct((m - split_rows, d), F32)]
    else:
        out_specs = [row]
        out_shape = [jax.ShapeDtypeStruct((m, d), F32)]
    if with_h:
        out_specs.append(row)
        out_shape.append(jax.ShapeDtypeStruct((m, d), BF16))
    outs = pl.pallas_call(
        functools.partial(_resnorm_kernel, n_first=n_first, split_in=split_in, split_out=split_out, with_h=with_h),
        grid=(m // tm,), in_specs=in_specs, out_specs=out_specs, out_shape=out_shape,
        compiler_params=_params(("arbitrary",)), name="resnorm",
    )(*args)
    xo = tuple(outs[:2]) if split_out else outs[0]
    return xo, (outs[-1] if with_h else None)


def _latnorm_kernel(q_ref, c_ref, gq_ref, gc_ref, _p, _s, cp_ref, cs_ref, qn_ref, cb_ref, *, n_first):
    qn_ref[...] = _rms(q_ref[...], gq_ref[...]).astype(BF16)
    cn = _rms(c_ref[...], gc_ref[...])
    cb_ref[...] = cn.astype(BF16)
    _store_split(cn, cp_ref, cs_ref, n_first)


def _store_split(val, first_ref, second_ref, n_first):
    @pl.when(pl.program_id(0) < n_first)
    def _():
        first_ref[...] = val

    @pl.when(pl.program_id(0) >= n_first)
    def _():
        second_ref[...] = val


def _layer_split_specs(tm, width, l, n_first, n_second):
    return [pl.BlockSpec((tm, width), lambda i, *_: (l * n_first + jnp.minimum(i, n_first - 1), 0)),
            pl.BlockSpec((tm, width), lambda i, *_: (l * n_second + jnp.maximum(i - n_first, 0), 0))]


def _layer_split_io(prev, n_in):
    shapes = [jax.ShapeDtypeStruct(p.shape, p.dtype) for p in prev]
    return shapes, [pl.BlockSpec(memory_space=pl.ANY)] * 2, {n_in: 0, n_in + 1: 1}


def latnorm(rest, gq, gc, l, prev, *, rows_first, tm=512):
    m = rest.shape[0]
    qb = OFF_Q // MLA_Q_RANK
    cb = OFF_CKV // MLA_KV_RANK
    n_first, n_second = rows_first // tm, (m - rows_first) // tm
    shapes, extra_specs, aliases = _layer_split_io(prev, 4)
    outs = pl.pallas_call(
        functools.partial(_latnorm_kernel, n_first=n_first),
        grid=(m // tm,),
        in_specs=[pl.BlockSpec((tm, MLA_Q_RANK), lambda i: (i, qb)),
                  pl.BlockSpec((tm, MLA_KV_RANK), lambda i: (i, cb)),
                  pl.BlockSpec((1, MLA_Q_RANK), lambda i: (0, 0)),
                  pl.BlockSpec((1, MLA_KV_RANK), lambda i: (0, 0))] + extra_specs,
        out_specs=_layer_split_specs(tm, MLA_KV_RANK, l, n_first, n_second)
        + [pl.BlockSpec((tm, MLA_Q_RANK), lambda i: (i, 0)),
           pl.BlockSpec((tm, MLA_KV_RANK), lambda i: (i, 0))],
        out_shape=shapes + [jax.ShapeDtypeStruct((m, MLA_Q_RANK), BF16),
                            jax.ShapeDtypeStruct((m, MLA_KV_RANK), BF16)],
        input_output_aliases=aliases,
        compiler_params=_params(("arbitrary",)), name="latnorm",
    )(rest, rest, gq.reshape(1, -1), gc.reshape(1, -1), *prev)
    return outs[2], outs[3], (outs[0], outs[1])


def _mm_kernel(a_ref, w_ref, o_ref, *, gelu_tiles, wt):
    r = (_dot_nt if wt else _dot)(a_ref[...], w_ref[...].astype(BF16))
    if gelu_tiles:
        @pl.when(pl.program_id(1) < gelu_tiles)
        def _():
            o_ref[...] = jax.nn.gelu(r).astype(o_ref.dtype)

        @pl.when(pl.program_id(1) >= gelu_tiles)
        def _():
            o_ref[...] = r.astype(o_ref.dtype)
    else:
        o_ref[...] = r.astype(o_ref.dtype)


def matmul(a, w, l, *, n=None, gelu_cols=0, wt=False, out_dtype=F32, tm=1024, tn=512, name="mm"):
    m, k = a.shape
    n = w.shape[1 if wt else 2] if n is None else n
    wspec = (pl.BlockSpec((None, tn, k), lambda i, j: (l, j, 0)) if wt
             else pl.BlockSpec((None, k, tn), lambda i, j: (l, 0, j)))
    return pl.pallas_call(
        functools.partial(_mm_kernel, gelu_tiles=gelu_cols // tn, wt=wt),
        grid=(m // tm, n // tn),
        in_specs=[pl.BlockSpec((tm, k), lambda i, j: (i, 0)), wspec],
        out_specs=pl.BlockSpec((tm, tn), lambda i, j: (i, j)),
        out_shape=jax.ShapeDtypeStruct((m, n), out_dtype),
        compiler_params=_params(("parallel", "arbitrary")), name=name,
    )(a, w)


def _ffn_up_kernel(h_ref, wg_ref, wu_ref, o_ref):
    h = h_ref[...]
    g = _dot(h, wg_ref[...].astype(BF16))
    u = _dot(h, wu_ref[...].astype(BF16))
    o_ref[...] = (jax.nn.silu(g) * u).astype(o_ref.dtype)


def ffn_up(h, wg, wu, l, tm=1536, tn=FF_TILE):
    m, k = h.shape
    n = wg.shape[2]
    wspec = pl.BlockSpec((None, k, tn), lambda i, j: (l, 0, j))
    return pl.pallas_call(
        _ffn_up_kernel,
        grid=(m // tm, n // tn),
        in_specs=[pl.BlockSpec((tm, k), lambda i, j: (i, 0)), wspec, wspec],
        out_specs=pl.BlockSpec((tm, tn), lambda i, j: (i, j)),
        out_shape=jax.ShapeDtypeStruct((m, n), BF16),
        compiler_params=_params(("parallel", "arbitrary")), name="ffn_up",
    )(h, wg, wu)


def _q_up_kernel(a_ref, w_ref, c_ref, s_ref, o_ref, *, scale):
    r = _dot(a_ref[...], w_ref[...]) * scale
    c = c_ref[...]
    s = s_ref[...]
    for h in range(r.shape[1] // Q_HEAD_PAD):
        lo = h * Q_HEAD_PAD
        o_ref[h, :, 0:MLA_NOPE_DIM] = r[:, lo:lo + MLA_NOPE_DIM].astype(o_ref.dtype)
        b = r[:, lo + MLA_NOPE_DIM:lo + Q_HEAD_PAD]
        o_ref[h, :, MLA_NOPE_DIM:Q_HEAD_PAD] = _rope_lanes(b, c, s).astype(o_ref.dtype)


Q_HEADS_PER_TILE = 4


def q_up(a, w, l, cos_t, sin_t, *, scale, tm=1024):
    m, k = a.shape
    n = w.shape[2]
    tn = Q_HEADS_PER_TILE * Q_HEAD_PAD
    return pl.pallas_call(
        functools.partial(_q_up_kernel, scale=scale),
        grid=(m // tm, n // tn),
        in_specs=[pl.BlockSpec((tm, k), lambda i, j: (i, 0)),
                  pl.BlockSpec((None, k, tn), lambda i, j: (l, 0, j)),
                  pl.BlockSpec((tm, LANES), lambda i, j: (i, 0)),
                  pl.BlockSpec((tm, LANES), lambda i, j: (i, 0))],
        out_specs=pl.BlockSpec((Q_HEADS_PER_TILE, tm, Q_HEAD_PAD), lambda i, j: (j, i, 0)),
        out_shape=jax.ShapeDtypeStruct((n // Q_HEAD_PAD, m, Q_HEAD_PAD), BF16),
        compiler_params=_params(("parallel", "arbitrary")), name="q_up",
    )(a, w, cos_t, sin_t)


def _kr_kernel(a_ref, w_ref, c_ref, s_ref, _p, _s, kp_ref, ks_ref, o_ref, *, n_first):
    r = _dot_nt(a_ref[...], w_ref[...].astype(BF16))
    lane = lax.broadcasted_iota(jnp.int32, r.shape, 1)
    half = MLA_ROPE_DIM // 2
    partner = jnp.where(lane < half, pltpu.roll(r, LANES - half, axis=1), pltpu.roll(r, half, axis=1))
    kr = jnp.where(lane < MLA_ROPE_DIM, r * c_ref[...] + partner * s_ref[...], 0.0)
    o_ref[...] = kr
    _store_split(kr[:, :MLA_ROPE_DIM], kp_ref, ks_ref, n_first)


def kr_proj(h, w_in_t, l, cos_t, sin_t, prev, *, rows_first, tm=1024):
    m, k = h.shape
    n_first, n_second = rows_first // tm, (m - rows_first) // tm
    shapes, extra_specs, aliases = _layer_split_io(prev, 4)
    outs = pl.pallas_call(
        functools.partial(_kr_kernel, n_first=n_first),
        grid=(m // tm,),
        in_specs=[pl.BlockSpec((tm, k), lambda i: (i, 0)),
                  pl.BlockSpec((None, LANES, k), lambda i: (l, OFF_KR // LANES, 0)),
                  pl.BlockSpec((tm, LANES), lambda i: (i, 0)),
                  pl.BlockSpec((tm, LANES), lambda i: (i, 0))] + extra_specs,
        out_specs=_layer_split_specs(tm, MLA_ROPE_DIM, l, n_first, n_second)
        + [pl.BlockSpec((tm, LANES), lambda i: (i, 0))],
        out_shape=shapes + [jax.ShapeDtypeStruct((m, LANES), F32)],
        input_output_aliases=aliases,
        compiler_params=_params(("arbitrary",)), name="in_kr",
    )(h, w_in_t, cos_t, sin_t, *prev)
    return outs[2], (outs[0], outs[1])


def _mix_out_kernel(a_ref, b_ref, c_ref, wa_ref, wb_ref, wc_ref, o_ref):
    o_ref[...] = (_dot(a_ref[...], wa_ref[...].astype(BF16)) + _dot(b_ref[...], wb_ref[...].astype(BF16))
                  + _dot(c_ref[...], wc_ref[...].astype(BF16))).astype(o_ref.dtype)


def mix_out(a, b, c, w_out, l, tm=1024, tn=512):
    m = a.shape[0]
    n = w_out.shape[2]
    ka, kb, kc = a.shape[1], b.shape[1], c.shape[1]
    return pl.pallas_call(
        _mix_out_kernel,
        grid=(m // tm, n // tn),
        in_specs=[pl.BlockSpec((tm, ka), lambda i, j: (i, 0)),
                  pl.BlockSpec((tm, kb), lambda i, j: (i, 0)),
                  pl.BlockSpec((tm, kc), lambda i, j: (i, 0)),
                  pl.BlockSpec((None, ka, tn), lambda i, j: (l, 0, j)),
                  pl.BlockSpec((None, kb, tn), lambda i, j: (l, ka // kb, j)),
                  pl.BlockSpec((None, kc, tn), lambda i, j: (l, (ka + kb) // kc, j))],
        out_specs=pl.BlockSpec((tm, tn), lambda i, j: (i, j)),
        out_shape=jax.ShapeDtypeStruct((m, n), BF16),
        compiler_params=_params(("parallel", "arbitrary")), name="mix_out",
    )(a, b, c, w_out, w_out, w_out)


def _chunk_mlp_kernel(u_ref, v_ref, g_ref, b_ref, w_ref, bs_ref, _, a_ref, vn_ref, *, first_sample_block):
    is_sample = pl.program_id(0) >= first_sample_block
    ri = lax.broadcasted_iota(jnp.int32, (CM_CHUNK, CM_CHUNK), 0) // CHUNK
    ci = lax.broadcasted_iota(jnp.int32, (CM_CHUNK, CM_CHUNK), 1) // CHUNK
    max_lag = jnp.where(is_sample, 0, 1)
    keep = (ri - ci >= 0) & (ri - ci <= max_lag)
    bs = bs_ref[0]
    ws = [jnp.where(keep, w_ref[0, h], 0.0).astype(BF16) for h in range(CM_HEADS)]
    for c in range(u_ref.shape[0] // CM_CHUNK):
        rows = slice(c * CM_CHUNK, (c + 1) * CM_CHUNK)
        v = v_ref[rows, :]
        mu = jnp.mean(v, axis=-1, keepdims=True)
        d = v - mu
        var = jnp.mean(d * d, axis=-1, keepdims=True)
        vn = (d * lax.rsqrt(var + EPS)) * g_ref[...] + b_ref[...]
        @pl.when(is_sample)
        def _():
            vn_ref[rows, :] = vn

        vb = vn.astype(BF16)
        for h in range(CM_HEADS):
            cols = slice(h * CM_HEAD_DIM, (h + 1) * CM_HEAD_DIM)
            sv = _dot(ws[h], vb[:, cols]) + bs[:, h:h + 1]
            a_ref[rows, cols] = (u_ref[rows, cols] * sv).astype(a_ref.dtype)


CM_ROWS = 4 * CM_CHUNK


def chunk_mlp(uv, ln_g, ln_b, ws2, bs2, l, first_sample_row, prev):
    m = uv.shape[0]
    first_sample_block = first_sample_row // CM_ROWS
    n_sample = m // CM_ROWS - first_sample_block
    sel = lambda c: jnp.where(c >= first_sample_block, 1, 0)
    return pl.pallas_call(
        functools.partial(_chunk_mlp_kernel, first_sample_block=first_sample_block),
        grid=(m // CM_ROWS,),
        in_specs=[pl.BlockSpec((CM_ROWS, D_CM), lambda c: (c, 0)),
                  pl.BlockSpec((CM_ROWS, D_CM), lambda c: (c, 1)),
                  pl.BlockSpec((1, D_CM), lambda c: (0, 0)),
                  pl.BlockSpec((1, D_CM), lambda c: (0, 0)),
                  pl.BlockSpec((None, 1, CM_HEADS, CM_CHUNK, CM_CHUNK), lambda c: (l, sel(c), 0, 0, 0)),
                  pl.BlockSpec((None, 1, CM_CHUNK, CM_HEADS), lambda c: (l, sel(c), 0, 0)),
                  pl.BlockSpec(memory_space=pl.ANY)],
        out_specs=[pl.BlockSpec((CM_ROWS, D_CM), lambda c: (c, 0)),
                   pl.BlockSpec((CM_ROWS, D_CM),
                                lambda c: (l * n_sample + jnp.maximum(c - first_sample_block, 0), 0))],
        out_shape=[jax.ShapeDtypeStruct((m, D_CM), BF16), jax.ShapeDtypeStruct(prev.shape, prev.dtype)],
        input_output_aliases={6: 1},
        compiler_params=_params(("arbitrary",)), name="chunk_mlp",
    )(uv, uv, ln_g.reshape(1, -1), ln_b.reshape(1, -1), ws2, bs2, prev)


def _pool_kernel(x_ref, hist_ref, w_ref, sc_ref, _, o_ref, e_ref, *, t_rows, pos0):
    t = pl.program_id(1)

    @pl.when(t == 0)
    def _():
        e_ref[0:HIST_ROWS, :] = hist_ref[0]

    x = x_ref[...]
    e_ref[HIST_ROWS:HIST_ROWS + t_rows, :] = x
    pos = pos0 + t * t_rows + lax.broadcasted_iota(jnp.int32, (t_rows, 1), 0)
    for g, w in enumerate(POOL_WINDOWS):
        cols = slice(g * POOL_GROUP_DIM, (g + 1) * POOL_GROUP_DIM)
        acc = x[:, cols]
        for k in range(1, w):
            acc = acc + e_ref[HIST_ROWS - k:HIST_ROWS - k + t_rows, cols]
        cnt = jnp.minimum(w, pos + 1).astype(F32)
        mg = (acc / cnt - x[:, cols]).astype(BF16)
        y = _dot(mg, w_ref[g]) * sc_ref[:, cols]
        o_ref[:, cols] = y.astype(o_ref.dtype)
    e_ref[0:HIST_ROWS, :] = e_ref[t_rows:t_rows + HIST_ROWS, :]


def pool_mix(rest, hist, hl, pool_w, pool_scale, l, prev, *, row0, n_seq, seq_len, t_rows, pos0):
    n_t = seq_len // t_rows
    b0 = row0 // t_rows
    return pl.pallas_call(
        functools.partial(_pool_kernel, t_rows=t_rows, pos0=pos0),
        grid=(n_seq, n_t),
        in_specs=[pl.BlockSpec((t_rows, D_POOL), lambda s, t: (b0 + s * n_t + t, OFF_POOL // D_POOL)),
                  pl.BlockSpec((None, 1, HIST_ROWS, D_POOL), lambda s, t: (hl, s, 0, 0)),
                  pl.BlockSpec((None,) + pool_w.shape[1:], lambda s, t: (l, 0, 0, 0)),
                  pl.BlockSpec((1, D_POOL), lambda s, t: (0, 0)),
                  pl.BlockSpec(memory_space=pl.ANY)],
        out_specs=pl.BlockSpec((t_rows, D_POOL), lambda s, t: (b0 + s * n_t + t, 0)),
        out_shape=jax.ShapeDtypeStruct(prev.shape, prev.dtype),
        scratch_shapes=[pltpu.VMEM((HIST_ROWS + t_rows, D_POOL), F32)],
        input_output_aliases={4: 0},
        compiler_params=_params(("parallel", "arbitrary")), name="pool_mix",
    )(rest, hist, pool_w, pool_scale.reshape(1, -1), prev)


KV_HEADS_PER_TILE = 8


def _kv_up_k_kernel(c_ref, w_ref, kr_ref, o_ref):
    r = _dot(c_ref[...], w_ref[...])
    kr = kr_ref[...].astype(BF16)
    for hh in range(KV_HEADS_PER_TILE):
        o_ref[hh, :, 0:MLA_NOPE_DIM] = r[:, hh * MLA_NOPE_DIM:(hh + 1) * MLA_NOPE_DIM].astype(BF16)
        o_ref[hh, :, MLA_NOPE_DIM:Q_HEAD_PAD] = kr


def _kv_up_v_kernel(c_ref, w_ref, o_ref):
    r = _dot(c_ref[...], w_ref[...])
    for hh in range(KV_HEADS_PER_TILE):
        o_ref[hh] = r[:, hh * MLA_V_DIM:(hh + 1) * MLA_V_DIM].astype(BF16)


def kv_up(ckv_b, kr, wkv2, l, rows, tm=1024):
    kd = ckv_b.shape[1]
    tn = KV_HEADS_PER_TILE * MLA_NOPE_DIM
    nj = MLA_HEADS // KV_HEADS_PER_TILE
    k = pl.pallas_call(
        _kv_up_k_kernel,
        grid=(rows // tm, nj),
        in_specs=[pl.BlockSpec((tm, kd), lambda i, j: (i, 0)),
                  pl.BlockSpec((None, kd, tn), lambda i, j: (l, 0, j)),
                  pl.BlockSpec((tm, LANES), lambda i, j: (i, 0))],
        out_specs=pl.BlockSpec((KV_HEADS_PER_TILE, tm, Q_HEAD_PAD), lambda i, j: (j, i, 0)),
        out_shape=jax.ShapeDtypeStruct((MLA_HEADS, rows, Q_HEAD_PAD), BF16),
        compiler_params=_params(("parallel", "arbitrary")), name="kv_up_k",
    )(ckv_b, wkv2, kr)
    v = pl.pallas_call(
        _kv_up_v_kernel,
        grid=(rows // tm, nj),
        in_specs=[pl.BlockSpec((tm, kd), lambda i, j: (i, 0)),
                  pl.BlockSpec((None, kd, tn), lambda i, j: (l, 0, j + nj))],
        out_specs=pl.BlockSpec((KV_HEADS_PER_TILE, tm, MLA_V_DIM), lambda i, j: (j, i, 0)),
        out_shape=jax.ShapeDtypeStruct((MLA_HEADS, rows, MLA_V_DIM), BF16),
        compiler_params=_params(("parallel", "arbitrary")), name="kv_up_v",
    )(ckv_b, wkv2)
    return k, v


def _attn_prompt_kernel(q_ref, k_ref, v_ref, _, o_ref, m_ref, l_ref, acc_ref, *, t):
    qi = pl.program_id(1)
    ki = pl.program_id(2)

    @pl.when(ki == 0)
    def _():
        m_ref[...] = jnp.full_like(m_ref, -jnp.inf)
        l_ref[...] = jnp.zeros_like(l_ref)
        acc_ref[...] = jnp.zeros_like(acc_ref)

    def all_heads(bias):
        def head(h, carry):
            s = _dot_nt(q_ref[h], k_ref[h])
            if bias is not None:
                s = s + bias
            m_prev = m_ref[h]
            m_new = jnp.maximum(m_prev, jnp.max(s, axis=-1, keepdims=True))
            alpha = jnp.exp2(m_prev - m_new)
            p = jnp.exp2(s - jnp.tile(m_new, (1, t // LANES)))
            l_ref[h] = alpha * l_ref[h] + jnp.sum(p, axis=-1, keepdims=True)
            acc_ref[h] = alpha * acc_ref[h] + _dot(p.astype(BF16), v_ref[h])
            m_ref[h] = m_new
            return carry
        lax.fori_loop(0, MLA_HEADS, head, 0, unroll=4)

    @pl.when(ki < qi)
    def _():
        all_heads(None)

    @pl.when(ki == qi)
    def _():
        rc = lax.broadcasted_iota(jnp.int32, (t, t), 0) // CHUNK
        cc = lax.broadcasted_iota(jnp.int32, (t, t), 1) // CHUNK
        all_heads(jnp.where(cc <= rc, 0.0, -jnp.inf))

    @pl.when(ki == pl.num_programs(2) - 1)
    def _():
        for h in range(MLA_HEADS):
            o_ref[:, h * MLA_V_DIM:(h + 1) * MLA_V_DIM] = (acc_ref[h] / l_ref[h]).astype(o_ref.dtype)


def attention_prompt(q, k, v, prev, *, n_batch, seq_len, t=512):
    nt = seq_len // t
    kmap = lambda b, qi, ki: (0, b * nt + jnp.minimum(ki, qi), 0)
    return pl.pallas_call(
        functools.partial(_attn_prompt_kernel, t=t),
        grid=(n_batch, nt, nt),
        in_specs=[pl.BlockSpec((MLA_HEADS, t, Q_HEAD_PAD), lambda b, qi, ki: (0, b * nt + qi, 0)),
                  pl.BlockSpec((MLA_HEADS, t, Q_HEAD_PAD), kmap),
                  pl.BlockSpec((MLA_HEADS, t, MLA_V_DIM), kmap),
                  pl.BlockSpec(memory_space=pl.ANY)],
        out_specs=pl.BlockSpec((t, D_MLA), lambda b, qi, ki: (b * nt + qi, 0)),
        out_shape=jax.ShapeDtypeStruct(prev.shape, prev.dtype),
        input_output_aliases={3: 0},
        scratch_shapes=[pltpu.VMEM((MLA_HEADS, t, LANES), F32),
                        pltpu.VMEM((MLA_HEADS, t, LANES), F32),
                        pltpu.VMEM((MLA_HEADS, t, MLA_V_DIM), F32)],
        compiler_params=_params(("parallel", "parallel", "arbitrary")), name="attn_prompt",
    )(q, k, v, prev)


def _attn_sample_kernel(q_ref, cp_ref, kp_ref, cn_ref, kn_ref, wk_ref, wv_ref, _, o_ref,
                        qa_ref, qr_ref, ol_ref, *, ls):
    for h in range(MLA_HEADS):
        rows = slice(h * ls, (h + 1) * ls)
        wk_h = wk_ref[:, h * MLA_NOPE_DIM:(h + 1) * MLA_NOPE_DIM]
        qa_ref[rows, :] = _dot_nt(q_ref[h, :, 0:MLA_NOPE_DIM], wk_h).astype(BF16)
        qr_ref[rows, :] = q_ref[h, :, MLA_NOPE_DIM:Q_HEAD_PAD]
    cp = cp_ref[...].astype(BF16)
    kp = kp_ref[...].astype(BF16)
    cn = cn_ref[...]
    kn = kn_ref[...].astype(BF16)
    qa = qa_ref[...]
    qr = qr_ref[...]
    s_p = _dot_nt(qa, cp) + _dot_nt(qr[:, :MLA_ROPE_DIM], kp)
    s_n = _dot_nt(qa, cn) + _dot_nt(qr, kn)
    m = jnp.maximum(jnp.max(s_p, axis=-1, keepdims=True), jnp.max(s_n, axis=-1, keepdims=True))
    p_p = jnp.exp2(s_p - m)
    p_n = jnp.exp2(s_n - m)
    den = jnp.sum(p_p, axis=-1, keepdims=True) + jnp.sum(p_n, axis=-1, keepdims=True)
    ol = (_dot(p_p.astype(BF16), cp) + _dot(p_n.astype(BF16), cn)) / den
    ol_ref[...] = ol.astype(BF16)
    for h in range(MLA_HEADS):
        wv_h = wv_ref[:, h * MLA_V_DIM:(h + 1) * MLA_V_DIM]
        o_ref[:, h * MLA_V_DIM:(h + 1) * MLA_V_DIM] = _dot(ol_ref[h * ls:(h + 1) * ls, :], wv_h).astype(o_ref.dtype)


def attention_sample(q, cache_ckv, cache_kr, ckv_b, kr, wkv2, l, prev, *, row0, n_batch, ls):
    past = cache_ckv.shape[2]
    rb0 = row0 // ls
    rmap = lambda b: (rb0 + b, 0)
    hk = MLA_HEADS * MLA_NOPE_DIM
    return pl.pallas_call(
        functools.partial(_attn_sample_kernel, ls=ls),
        grid=(n_batch,),
        in_specs=[pl.BlockSpec((MLA_HEADS, ls, Q_HEAD_PAD), lambda b: (0, rb0 + b, 0)),
                  pl.BlockSpec((None, None, past, MLA_KV_RANK), lambda b: (l, b, 0, 0)),
                  pl.BlockSpec((None, None, past, MLA_ROPE_DIM), lambda b: (l, b, 0, 0)),
                  pl.BlockSpec((ls, MLA_KV_RANK), rmap),
                  pl.BlockSpec((ls, LANES), rmap),
                  pl.BlockSpec((None, MLA_KV_RANK, hk), lambda b: (l, 0, 0)),
                  pl.BlockSpec((None, MLA_KV_RANK, hk), lambda b: (l, 0, 1)),
                  pl.BlockSpec(memory_space=pl.ANY)],
        out_specs=pl.BlockSpec((ls, D_MLA), rmap),
        out_shape=jax.ShapeDtypeStruct(prev.shape, prev.dtype),
        scratch_shapes=[pltpu.VMEM((MLA_HEADS * ls, MLA_KV_RANK), BF16),
                        pltpu.VMEM((MLA_HEADS * ls, LANES), BF16),
                        pltpu.VMEM((MLA_HEADS * ls, MLA_KV_RANK), BF16)],
        input_output_aliases={7: 0},
        compiler_params=_params(("parallel",)), name="attn_sample",
    )(q, cache_ckv, cache_kr, ckv_b, kr, wkv2, wkv2, prev)


def _rope_tables(pos):
    half = MLA_ROPE_DIM // 2
    inv = ROPE_THETA ** (-jnp.arange(half, dtype=F32) / half)
    ang = pos.astype(F32)[:, None] * inv[None, :]
    cos, sin = jnp.cos(ang), jnp.sin(ang)
    z = jnp.zeros((pos.shape[0], LANES - MLA_ROPE_DIM), F32)
    return jnp.concatenate([cos, cos, z], axis=1), jnp.concatenate([-sin, sin, z], axis=1)


def _swap_halves(w):
    half = w.shape[-1] // 2
    return jnp.concatenate([w[..., half:], w[..., :half]], axis=-1)


def kernel(x_prompt, x_sample, cache_mla_ckv, cache_mla_krope, state_pool, g_pre_mix, g_post_mix, g_pre_ffn, g_post_ffn, w_in, cm_ln_g, cm_ln_b, cm_ws, cm_bs, pool_w, pool_scale, mla_q_norm, mla_w_uq, mla_kv_norm, mla_w_ukv, w_out, ffn_w_gate, ffn_w_up, ffn_w_down):
    bp, lp, d = x_prompt.shape
    bs_, ls, _ = x_sample.shape
    past = cache_mla_ckv.shape[2]
    mp, ms = bp * lp, bs_ * ls
    m = mp + ms

    x = (x_prompt.reshape(mp, d), x_sample.reshape(ms, d))
    pos = jnp.concatenate([jnp.tile(jnp.arange(lp), bp), jnp.tile(past + jnp.arange(ls), bs_)])
    cos_t, sin_t = _rope_tables(pos)

    w_in_t = jnp.swapaxes(w_in, 1, 2)
    wq = mla_w_uq.reshape(DEPTH, MLA_Q_RANK, MLA_HEADS, MLA_NOPE_DIM + MLA_ROPE_DIM)
    wq_rope = wq[..., MLA_NOPE_DIM:]
    wq2 = jnp.concatenate([wq[..., :MLA_NOPE_DIM], wq_rope, _swap_halves(wq_rope)], axis=-1)
    wq2 = wq2.reshape(DEPTH, MLA_Q_RANK, MLA_HEADS * Q_HEAD_PAD).astype(BF16)
    wkv = mla_w_ukv.reshape(DEPTH, MLA_KV_RANK, MLA_HEADS, MLA_NOPE_DIM + MLA_V_DIM)
    wkv2 = jnp.concatenate([wkv[..., :MLA_NOPE_DIM].reshape(DEPTH, MLA_KV_RANK, -1),
                            wkv[..., MLA_NOPE_DIM:].reshape(DEPTH, MLA_KV_RANK, -1)], axis=-1).astype(BF16)
    wd_b = ffn_w_down.astype(BF16)
    pool_w_b = pool_w.astype(BF16)
    ws_s = cm_ws.at[:, :, CHUNK:, CHUNK:].set(cm_ws[:, :, :CHUNK, :CHUNK])
    ws2 = jnp.stack([cm_ws, ws_s], axis=1)
    bs_s = jnp.concatenate([cm_bs[:, :, :CHUNK], cm_bs[:, :, :CHUNK]], axis=-1)
    bs2 = jnp.swapaxes(jnp.stack([cm_bs, bs_s], axis=1), -1, -2)
    hist_p = jnp.zeros((1, bp, HIST_ROWS, D_POOL), F32)
    hist_s = jnp.pad(state_pool, ((0, 0), (0, 0), (HIST_ROWS - POOL_HIST, 0), (0, 0)))

    outs = {k: [] for k in ("pool_p", "pool_s")}
    ckv_all = (jnp.zeros((DEPTH * mp, MLA_KV_RANK), F32), jnp.zeros((DEPTH * ms, MLA_KV_RANK), F32))
    kr_all = (jnp.zeros((DEPTH * mp, MLA_ROPE_DIM), F32), jnp.zeros((DEPTH * ms, MLA_ROPE_DIM), F32))
    cmv_all = jnp.zeros((DEPTH * ms, D_CM), F32)
    b = jnp.zeros((m, D_POOL), BF16)
    c = jnp.zeros((m, D_MLA), BF16)
    h = prenorm(x[0], x[1], g_pre_mix[0])
    for l in range(DEPTH):
        last = l + 1 == DEPTH
        rest = matmul(h, w_in_t, l, n=OFF_KR, gelu_cols=OFF_POOL, wt=True, name="in_proj")
        kr, kr_all = kr_proj(h, w_in_t, l, cos_t, sin_t, kr_all, rows_first=mp)
        a, cmv_all = chunk_mlp(rest, cm_ln_g[l], cm_ln_b[l], ws2, bs2, l, mp, cmv_all)
        b = pool_mix(rest, hist_p, 0, pool_w_b, pool_scale[l], l, b, row0=0, n_seq=bp, seq_len=lp,
                     t_rows=512, pos0=0)
        b = pool_mix(rest, hist_s, l, pool_w_b, pool_scale[l], l, b, row0=mp, n_seq=bs_, seq_len=ls,
                     t_rows=ls, pos0=past)
        qn, ckv_b, ckv_all = latnorm(rest, mla_q_norm[l], mla_kv_norm[l], l, ckv_all, rows_first=mp)
        q = q_up(qn, wq2, l, cos_t, sin_t, scale=Q_SCALE)
        k_p, v_p = kv_up(ckv_b, kr, wkv2, l, mp)
        c = attention_prompt(q, k_p, v_p, c, n_batch=bp, seq_len=lp)
        c = attention_sample(q, cache_mla_ckv, cache_mla_krope, ckv_b, kr, wkv2, l, c,
                             row0=mp, n_batch=bs_, ls=ls)
        mix = mix_out(a, b, c, w_out, l)
        x, h2 = resnorm(x, mix, g_post_mix[l], g_pre_ffn[l])
        act = ffn_up(h2, ffn_w_gate, ffn_w_up, l)
        f = matmul(act, wd_b, l, out_dtype=BF16, tm=512, name="ffn_down")
        x, h = resnorm(x, f, g_post_ffn[l], None if last else g_pre_mix[l + 1],
                       split_rows=mp, split_out=last)

        tail = rest.reshape(m // ls, ls, rest.shape[1])[:, ls - POOL_HIST:, OFF_POOL:OFF_POOL + D_POOL]
        outs["pool_p"].append(tail[lp // ls - 1:mp // ls:lp // ls])
        outs["pool_s"].append(tail[mp // ls:])

    return (x[0].reshape(bp, lp, d), x[1].reshape(bs_, ls, d),
            ckv_all[0].reshape(DEPTH, bp, lp, MLA_KV_RANK), kr_all[0].reshape(DEPTH, bp, lp, MLA_ROPE_DIM),
            jnp.stack(outs["pool_p"]),
            ckv_all[1].reshape(DEPTH, bs_, ls, MLA_KV_RANK), kr_all[1].reshape(DEPTH, bs_, ls, MLA_ROPE_DIM),
            jnp.stack(outs["pool_s"]), cmv_all.reshape(DEPTH, bs_, ls, D_CM))
```

```python
import functools

import jax
import jax.numpy as jnp
from jax import lax
from jax.experimental import pallas as pl
from jax.experimental.pallas import tpu as pltpu

D_MODEL = 4096
DEPTH = 4
CHUNK = 64
EPS = 1e-6
CM_CHUNK = 128
CM_HEADS = 8
CM_HEAD_DIM = 128
D_CM = CM_HEADS * CM_HEAD_DIM
POOL_WINDOWS = (2, 4, 8, 16)
POOL_GROUP_DIM = 256
D_POOL = len(POOL_WINDOWS) * POOL_GROUP_DIM
POOL_HIST = max(POOL_WINDOWS) - 1
HIST_ROWS = POOL_HIST + 1
MLA_HEADS = 16
MLA_Q_RANK = 1024
MLA_KV_RANK = 512
MLA_NOPE_DIM = 128
MLA_ROPE_DIM = 64
MLA_V_DIM = 128
D_MLA = MLA_HEADS * MLA_V_DIM
MLA_SCALE = (MLA_NOPE_DIM + MLA_ROPE_DIM) ** -0.5
Q_SCALE = MLA_SCALE * 1.4426950408889634
ROPE_THETA = 10000.0
Q_HEAD_PAD = 256
OFF_V = D_CM
OFF_POOL = 2 * D_CM
OFF_Q = OFF_POOL + D_POOL
OFF_CKV = OFF_Q + MLA_Q_RANK
OFF_KR = OFF_CKV + MLA_KV_RANK
FF_TILE = 256

LANES = 128
VMEM_LIMIT = 56 * 1024 * 1024

BF16 = jnp.bfloat16
F32 = jnp.float32


def _params(sem):
    return pltpu.CompilerParams(dimension_semantics=sem, vmem_limit_bytes=VMEM_LIMIT)


def _rms(x, g):
    r = lax.rsqrt(jnp.mean(x * x, axis=-1, keepdims=True) + EPS)
    return (x * r) * g


def _dot(a, b):
    return jnp.dot(a, b, preferred_element_type=F32)


def _dot_nt(a, b):
    return lax.dot_general(a, b, (((1,), (1,)), ((), ())), preferred_element_type=F32)


def _rope_lanes(b, c, s):
    return b * c + pltpu.roll(b, 64, axis=1) * s


def _two_source_specs(tm, d, n_first):
    return [pl.BlockSpec((tm, d), lambda i: (jnp.minimum(i, n_first - 1), 0)),
            pl.BlockSpec((tm, d), lambda i: (jnp.maximum(i - n_first, 0), 0))]


def _prenorm_kernel(xp_ref, xs_ref, g_ref, h_ref, *, n_first):
    x = jnp.where(pl.program_id(0) < n_first, xp_ref[...], xs_ref[...])
    h_ref[...] = _rms(x, g_ref[...]).astype(BF16)


def prenorm(xp, xs, g, tm=512):
    d = xp.shape[1]
    m = xp.shape[0] + xs.shape[0]
    n_first = xp.shape[0] // tm
    return pl.pallas_call(
        functools.partial(_prenorm_kernel, n_first=n_first),
        grid=(m // tm,),
        in_specs=_two_source_specs(tm, d, n_first) + [pl.BlockSpec((1, d), lambda i: (0, 0))],
        out_specs=pl.BlockSpec((tm, d), lambda i: (i, 0)),
        out_shape=jax.ShapeDtypeStruct((m, d), BF16),
        compiler_params=_params(("arbitrary",)),
        name="prenorm",
    )(xp, xs, g.reshape(1, d))


def _resnorm_kernel(*refs, n_first, split_in, split_out, with_h):
    refs = list(refs)
    if split_in:
        xp_ref, xs_ref = refs[:2]
        x = jnp.where(pl.program_id(0) < n_first, xp_ref[...], xs_ref[...])
        refs = refs[2:]
    else:
        x = refs[0][...]
        refs = refs[1:]
    f_ref, gpost_ref = refs[:2]
    refs = refs[2:]
    if with_h:
        gpre_ref = refs[0]
        refs = refs[1:]
    xn = x + _rms(f_ref[...].astype(F32), gpost_ref[...])
    if split_out:
        _store_split(xn, refs[0], refs[1], n_first)
        refs = refs[2:]
    else:
        refs[0][...] = xn
        refs = refs[1:]
    if with_h:
        refs[0][...] = _rms(xn, gpre_ref[...]).astype(BF16)


def resnorm(x, f, g_post, g_pre, *, split_rows=None, split_out=False):
    split_in = isinstance(x, (tuple, list))
    tm = 256 if split_in or split_out else 384
    m, d = f.shape
    with_h = g_pre is not None
    n_first = (x[0].shape[0] if split_in else split_rows if split_out else m) // tm
    row = pl.BlockSpec((tm, d), lambda i: (i, 0))
    vec = pl.BlockSpec((1, d), lambda i: (0, 0))
    in_specs = (_two_source_specs(tm, d, n_first) if split_in else [row]) + [row, vec]
    args = (list(x) if split_in else [x]) + [f, g_post.reshape(1, d)]
    if with_h:
        in_specs.append(vec)
        args.append(g_pre.reshape(1, d))
    if split_out:
        out_specs = _two_source_specs(tm, d, n_first)
        out_shape = [jax.ShapeDtypeStruct((split_rows, d), F32), jax.ShapeDtypeStruct((m - split_rows, d), F32)]
    else:
        out_specs = [row]
        out_shape = [jax.ShapeDtypeStruct((m, d), F32)]
    if with_h:
        out_specs.append(row)
        out_shape.append(jax.ShapeDtypeStruct((m, d), BF16))
    outs = pl.pallas_call(
        functools.partial(_resnorm_kernel, n_first=n_first, split_in=split_in, split_out=split_out, with_h=with_h),
        grid=(m // tm,), in_specs=in_specs, out_specs=out_specs, out_shape=out_shape,
        compiler_params=_params(("arbitrary",)), name="resnorm",
    )(*args)
    xo = tuple(outs[:2]) if split_out else outs[0]
    return xo, (outs[-1] if with_h else None)


def _latnorm_kernel(q_ref, c_ref, gq_ref, gc_ref, _p, _s, cp_ref, cs_ref, qn_ref, cb_ref, *, n_first):
    qn_ref[...] = _rms(q_ref[...], gq_ref[...]).astype(BF16)
    cn = _rms(c_ref[...], gc_ref[...])
    cb_ref[...] = cn.astype(BF16)
    _store_split(cn, cp_ref, cs_ref, n_first)


def _store_split(val, first_ref, second_ref, n_first):
    @pl.when(pl.program_id(0) < n_first)
    def _():
        first_ref[...] = val

    @pl.when(pl.program_id(0) >= n_first)
    def _():
        second_ref[...] = val


def _layer_split_specs(tm, width, l, n_first, n_second):
    return [pl.BlockSpec((tm, width), lambda i, *_: (l * n_first + jnp.minimum(i, n_first - 1), 0)),
            pl.BlockSpec((tm, width), lambda i, *_: (l * n_second + jnp.maximum(i - n_first, 0), 0))]


def _layer_split_io(prev, n_in):
    shapes = [jax.ShapeDtypeStruct(p.shape, p.dtype) for p in prev]
    return shapes, [pl.BlockSpec(memory_space=pl.ANY)] * 2, {n_in: 0, n_in + 1: 1}


def latnorm(rest, gq, gc, l, prev, *, rows_first, tm=512):
    m = rest.shape[0]
    qb = OFF_Q // MLA_Q_RANK
    cb = OFF_CKV // MLA_KV_RANK
    n_first, n_second = rows_first // tm, (m - rows_first) // tm
    shapes, extra_specs, aliases = _layer_split_io(prev, 4)
    outs = pl.pallas_call(
        functools.partial(_latnorm_kernel, n_first=n_first),
        grid=(m // tm,),
        in_specs=[pl.BlockSpec((tm, MLA_Q_RANK), lambda i: (i, qb)),
                  pl.BlockSpec((tm, MLA_KV_RANK), lambda i: (i, cb)),
                  pl.BlockSpec((1, MLA_Q_RANK), lambda i: (0, 0)),
                  pl.BlockSpec((1, MLA_KV_RANK), lambda i: (0, 0))] + extra_specs,
        out_specs=_layer_split_specs(tm, MLA_KV_RANK, l, n_first, n_second)
        + [pl.BlockSpec((tm, MLA_Q_RANK), lambda i: (i, 0)),
           pl.BlockSpec((tm, MLA_KV_RANK), lambda i: (i, 0))],
        out_shape=shapes + [jax.ShapeDtypeStruct((m, MLA_Q_RANK), BF16),
                            jax.ShapeDtypeStruct((m, MLA_KV_RANK), BF16)],
        input_output_aliases=aliases,
        compiler_params=_params(("arbitrary",)), name="latnorm",
    )(rest, rest, gq.reshape(1, -1), gc.reshape(1, -1), *prev)
    return outs[2], outs[3], (outs[0], outs[1])


def _mm_kernel(a_ref, w_ref, o_ref, *, gelu_tiles, wt):
    r = (_dot_nt if wt else _dot)(a_ref[...], w_ref[...].astype(BF16))
    if gelu_tiles:
        @pl.when(pl.program_id(1) < gelu_tiles)
        def _():
            o_ref[...] = jax.nn.gelu(r).astype(o_ref.dtype)

        @pl.when(pl.program_id(1) >= gelu_tiles)
        def _():
            o_ref[...] = r.astype(o_ref.dtype)
    else:
        o_ref[...] = r.astype(o_ref.dtype)


def matmul(a, w, l, *, n=None, gelu_cols=0, wt=False, out_dtype=F32, tm=1024, tn=512, name="mm"):
    m, k = a.shape
    n = w.shape[1 if wt else 2] if n is None else n
    wspec = (pl.BlockSpec((None, tn, k), lambda i, j: (l, j, 0)) if wt
             else pl.BlockSpec((None, k, tn), lambda i, j: (l, 0, j)))
    return pl.pallas_call(
        functools.partial(_mm_kernel, gelu_tiles=gelu_cols // tn, wt=wt),
        grid=(m // tm, n // tn),
        in_specs=[pl.BlockSpec((tm, k), lambda i, j: (i, 0)), wspec],
        out_specs=pl.BlockSpec((tm, tn), lambda i, j: (i, j)),
        out_shape=jax.ShapeDtypeStruct((m, n), out_dtype),
        compiler_params=_params(("parallel", "arbitrary")), name=name,
    )(a, w)


def _ffn_up_kernel(h_ref, wg_ref, wu_ref, o_ref):
    h = h_ref[...]
    g = _dot(h, wg_ref[...].astype(BF16))
    u = _dot(h, wu_ref[...].astype(BF16))
    o_ref[...] = (jax.nn.silu(g) * u).astype(o_ref.dtype)


def ffn_up(h, wg, wu, l, tm=1536, tn=FF_TILE):
    m, k = h.shape
    n = wg.shape[2]
    wspec = pl.BlockSpec((None, k, tn), lambda i, j: (l, 0, j))
    return pl.pallas_call(
        _ffn_up_kernel,
        grid=(m // tm, n // tn),
        in_specs=[pl.BlockSpec((tm, k), lambda i, j: (i, 0)), wspec, wspec],
        out_specs=pl.BlockSpec((tm, tn), lambda i, j: (i, j)),
        out_shape=jax.ShapeDtypeStruct((m, n), BF16),
        compiler_params=_params(("parallel", "arbitrary")), name="ffn_up",
    )(h, wg, wu)


def _q_up_kernel(a_ref, w_ref, c_ref, s_ref, o_ref, *, scale):
    r = _dot(a_ref[...], w_ref[...]) * scale
    c = c_ref[...]
    s = s_ref[...]
    for h in range(r.shape[1] // Q_HEAD_PAD):
        lo = h * Q_HEAD_PAD
        o_ref[h, :, 0:MLA_NOPE_DIM] = r[:, lo:lo + MLA_NOPE_DIM].astype(o_ref.dtype)
        b = r[:, lo + MLA_NOPE_DIM:lo + Q_HEAD_PAD]
        o_ref[h, :, MLA_NOPE_DIM:Q_HEAD_PAD] = _rope_lanes(b, c, s).astype(o_ref.dtype)


Q_HEADS_PER_TILE = 4


def q_up(a, w, l, cos_t, sin_t, *, scale, tm=1024):
    m, k = a.shape
    n = w.shape[2]
    tn = Q_HEADS_PER_TILE * Q_HEAD_PAD
    return pl.pallas_call(
        functools.partial(_q_up_kernel, scale=scale),
        grid=(m // tm, n // tn),
        in_specs=[pl.BlockSpec((tm, k), lambda i, j: (i, 0)),
                  pl.BlockSpec((None, k, tn), lambda i, j: (l, 0, j)),
                  pl.BlockSpec((tm, LANES), lambda i, j: (i, 0)),
                  pl.BlockSpec((tm, LANES), lambda i, j: (i, 0))],
        out_specs=pl.BlockSpec((Q_HEADS_PER_TILE, tm, Q_HEAD_PAD), lambda i, j: (j, i, 0)),
        out_shape=jax.ShapeDtypeStruct((n // Q_HEAD_PAD, m, Q_HEAD_PAD), BF16),
        compiler_params=_params(("parallel", "arbitrary")), name="q_up",
    )(a, w, cos_t, sin_t)


def _kr_kernel(a_ref, w_ref, c_ref, s_ref, _p, _s, kp_ref, ks_ref, o_ref, *, n_first):
    r = _dot_nt(a_ref[...], w_ref[...].astype(BF16))
    lane = lax.broadcasted_iota(jnp.int32, r.shape, 1)
    half = MLA_ROPE_DIM // 2
    partner = jnp.where(lane < half, pltpu.roll(r, LANES - half, axis=1), pltpu.roll(r, half, axis=1))
    kr = jnp.where(lane < MLA_ROPE_DIM, r * c_ref[...] + partner * s_ref[...], 0.0)
    o_ref[...] = kr
    _store_split(kr[:, :MLA_ROPE_DIM], kp_ref, ks_ref, n_first)


def kr_proj(h, w_in_t, l, cos_t, sin_t, prev, *, rows_first, tm=1024):
    m, k = h.shape
    n_first, n_second = rows_first // tm, (m - rows_first) // tm
    shapes, extra_specs, aliases = _layer_split_io(prev, 4)
    outs = pl.pallas_call(
        functools.partial(_kr_kernel, n_first=n_first),
        grid=(m // tm,),
        in_specs=[pl.BlockSpec((tm, k), lambda i: (i, 0)),
                  pl.BlockSpec((None, LANES, k), lambda i: (l, OFF_KR // LANES, 0)),
                  pl.BlockSpec((tm, LANES), lambda i: (i, 0)),
                  pl.BlockSpec((tm, LANES), lambda i: (i, 0))] + extra_specs,
        out_specs=_layer_split_specs(tm, MLA_ROPE_DIM, l, n_first, n_second)
        + [pl.BlockSpec((tm, LANES), lambda i: (i, 0))],
        out_shape=shapes + [jax.ShapeDtypeStruct((m, LANES), F32)],
        input_output_aliases=aliases,
        compiler_params=_params(("arbitrary",)), name="in_kr",
    )(h, w_in_t, cos_t, sin_t, *prev)
    return outs[2], (outs[0], outs[1])


def _mix_out_kernel(a_ref, b_ref, c_ref, wa_ref, wb_ref, wc_ref, o_ref):
    o_ref[...] = (_dot(a_ref[...], wa_ref[...].astype(BF16)) + _dot(b_ref[...], wb_ref[...].astype(BF16))
                  + _dot(c_ref[...], wc_ref[...].astype(BF16))).astype(o_ref.dtype)


def mix_out(a, b, c, w_out, l, tm=1024, tn=512):
    m = a.shape[0]
    n = w_out.shape[2]
    ka, kb, kc = a.shape[1], b.shape[1], c.shape[1]
    return pl.pallas_call(
        _mix_out_kernel,
        grid=(m // tm, n // tn),
        in_specs=[pl.BlockSpec((tm, ka), lambda i, j: (i, 0)),
                  pl.BlockSpec((tm, kb), lambda i, j: (i, 0)),
                  pl.BlockSpec((tm, kc), lambda i, j: (i, 0)),
                  pl.BlockSpec((None, ka, tn), lambda i, j: (l, 0, j)),
                  pl.BlockSpec((None, kb, tn), lambda i, j: (l, ka // kb, j)),
                  pl.BlockSpec((None, kc, tn), lambda i, j: (l, (ka + kb) // kc, j))],
        out_specs=pl.BlockSpec((tm, tn), lambda i, j: (i, j)),
        out_shape=jax.ShapeDtypeStruct((m, n), BF16),
        compiler_params=_params(("parallel", "arbitrary")), name="mix_out",
    )(a, b, c, w_out, w_out, w_out)


def _chunk_mlp_kernel(u_ref, v_ref, g_ref, b_ref, w_ref, bs_ref, _, a_ref, vn_ref, *, first_sample_block):
    is_sample = pl.program_id(0) >= first_sample_block
    ri = lax.broadcasted_iota(jnp.int32, (CM_CHUNK, CM_CHUNK), 0) // CHUNK
    ci = lax.broadcasted_iota(jnp.int32, (CM_CHUNK, CM_CHUNK), 1) // CHUNK
    max_lag = jnp.where(is_sample, 0, 1)
    keep = (ri - ci >= 0) & (ri - ci <= max_lag)
    bs = bs_ref[0]
    ws = [jnp.where(keep, w_ref[0, h], 0.0).astype(BF16) for h in range(CM_HEADS)]
    for c in range(u_ref.shape[0] // CM_CHUNK):
        rows = slice(c * CM_CHUNK, (c + 1) * CM_CHUNK)
        v = v_ref[rows, :]
        mu = jnp.mean(v, axis=-1, keepdims=True)
        d = v - mu
        var = jnp.mean(d * d, axis=-1, keepdims=True)
        vn = (d * lax.rsqrt(var + EPS)) * g_ref[...] + b_ref[...]
        @pl.when(is_sample)
        def _():
            vn_ref[rows, :] = vn

        vb = vn.astype(BF16)
        for h in range(CM_HEADS):
            cols = slice(h * CM_HEAD_DIM, (h + 1) * CM_HEAD_DIM)
            sv = _dot(ws[h], vb[:, cols]) + bs[:, h:h + 1]
            a_ref[rows, cols] = (u_ref[rows, cols] * sv).astype(a_ref.dtype)


CM_ROWS = 4 * CM_CHUNK


def chunk_mlp(uv, ln_g, ln_b, ws2, bs2, l, first_sample_row, prev):
    m = uv.shape[0]
    first_sample_block = first_sample_row // CM_ROWS
    n_sample = m // CM_ROWS - first_sample_block
    sel = lambda c: jnp.where(c >= first_sample_block, 1, 0)
    return pl.pallas_call(
        functools.partial(_chunk_mlp_kernel, first_sample_block=first_sample_block),
        grid=(m // CM_ROWS,),
        in_specs=[pl.BlockSpec((CM_ROWS, D_CM), lambda c: (c, 0)),
                  pl.BlockSpec((CM_ROWS, D_CM), lambda c: (c, 1)),
                  pl.BlockSpec((1, D_CM), lambda c: (0, 0)),
                  pl.BlockSpec((1, D_CM), lambda c: (0, 0)),
                  pl.BlockSpec((None, 1, CM_HEADS, CM_CHUNK, CM_CHUNK), lambda c: (l, sel(c), 0, 0, 0)),
                  pl.BlockSpec((None, 1, CM_CHUNK, CM_HEADS), lambda c: (l, sel(c), 0, 0)),
                  pl.BlockSpec(memory_space=pl.ANY)],
        out_specs=[pl.BlockSpec((CM_ROWS, D_CM), lambda c: (c, 0)),
                   pl.BlockSpec((CM_ROWS, D_CM),
                                lambda c: (l * n_sample + jnp.maximum(c - first_sample_block, 0), 0))],
        out_shape=[jax.ShapeDtypeStruct((m, D_CM), BF16), jax.ShapeDtypeStruct(prev.shape, prev.dtype)],
        input_output_aliases={6: 1},
        compiler_params=_params(("arbitrary",)), name="chunk_mlp",
    )(uv, uv, ln_g.reshape(1, -1), ln_b.reshape(1, -1), ws2, bs2, prev)


def _pool_kernel(x_ref, hist_ref, w_ref, sc_ref, _, o_ref, e_ref, *, t_rows, pos0):
    t = pl.program_id(1)

    @pl.when(t == 0)
    def _():
        e_ref[0:HIST_ROWS, :] = hist_ref[0]

    x = x_ref[...]
    e_ref[HIST_ROWS:HIST_ROWS + t_rows, :] = x
    pos = pos0 + t * t_rows + lax.broadcasted_iota(jnp.int32, (t_rows, 1), 0)
    for g, w in enumerate(POOL_WINDOWS):
        cols = slice(g * POOL_GROUP_DIM, (g + 1) * POOL_GROUP_DIM)
        acc = x[:, cols]
        for k in range(1, w):
            acc = acc + e_ref[HIST_ROWS - k:HIST_ROWS - k + t_rows, cols]
        cnt = jnp.minimum(w, pos + 1).astype(F32)
        mg = (acc / cnt - x[:, cols]).astype(BF16)
        y = _dot(mg, w_ref[g]) * sc_ref[:, cols]
        o_ref[:, cols] = y.astype(o_ref.dtype)
    e_ref[0:HIST_ROWS, :] = e_ref[t_rows:t_rows + HIST_ROWS, :]


def pool_mix(rest, hist, hl, pool_w, pool_scale, l, prev, *, row0, n_seq, seq_len, t_rows, pos0):
    n_t = seq_len // t_rows
    b0 = row0 // t_rows
    return pl.pallas_call(
        functools.partial(_pool_kernel, t_rows=t_rows, pos0=pos0),
        grid=(n_seq, n_t),
        in_specs=[pl.BlockSpec((t_rows, D_POOL), lambda s, t: (b0 + s * n_t + t, OFF_POOL // D_POOL)),
                  pl.BlockSpec((None, 1, HIST_ROWS, D_POOL), lambda s, t: (hl, s, 0, 0)),
                  pl.BlockSpec((None,) + pool_w.shape[1:], lambda s, t: (l, 0, 0, 0)),
                  pl.BlockSpec((1, D_POOL), lambda s, t: (0, 0)),
                  pl.BlockSpec(memory_space=pl.ANY)],
        out_specs=pl.BlockSpec((t_rows, D_POOL), lambda s, t: (b0 + s * n_t + t, 0)),
        out_shape=jax.ShapeDtypeStruct(prev.shape, prev.dtype),
        scratch_shapes=[pltpu.VMEM((HIST_ROWS + t_rows, D_POOL), F32)],
        input_output_aliases={4: 0},
        compiler_params=_params(("parallel", "arbitrary")), name="pool_mix",
    )(rest, hist, pool_w, pool_scale.reshape(1, -1), prev)


KV_HEADS_PER_TILE = 8


def _kv_up_k_kernel(c_ref, w_ref, kr_ref, o_ref):
    r = _dot(c_ref[...], w_ref[...])
    kr = kr_ref[...].astype(BF16)
    for hh in range(KV_HEADS_PER_TILE):
        o_ref[hh, :, 0:MLA_NOPE_DIM] = r[:, hh * MLA_NOPE_DIM:(hh + 1) * MLA_NOPE_DIM].astype(BF16)
        o_ref[hh, :, MLA_NOPE_DIM:Q_HEAD_PAD] = kr


def _kv_up_v_kernel(c_ref, w_ref, o_ref):
    r = _dot(c_ref[...], w_ref[...])
    ones = jnp.ones((r.shape[0], MLA_V_DIM), BF16)
    for hh in range(KV_HEADS_PER_TILE):
        o_ref[hh, :, 0:MLA_V_DIM] = r[:, hh * MLA_V_DIM:(hh + 1) * MLA_V_DIM].astype(BF16)
        o_ref[hh, :, MLA_V_DIM:2 * MLA_V_DIM] = ones


def kv_up(ckv_b, kr, wkv2, l, rows, tm=1024):
    kd = ckv_b.shape[1]
    tn = KV_HEADS_PER_TILE * MLA_NOPE_DIM
    nj = MLA_HEADS // KV_HEADS_PER_TILE
    k = pl.pallas_call(
        _kv_up_k_kernel,
        grid=(rows // tm, nj),
        in_specs=[pl.BlockSpec((tm, kd), lambda i, j: (i, 0)),
                  pl.BlockSpec((None, kd, tn), lambda i, j: (l, 0, j)),
                  pl.BlockSpec((tm, LANES), lambda i, j: (i, 0))],
        out_specs=pl.BlockSpec((KV_HEADS_PER_TILE, tm, Q_HEAD_PAD), lambda i, j: (j, i, 0)),
        out_shape=jax.ShapeDtypeStruct((MLA_HEADS, rows, Q_HEAD_PAD), BF16),
        compiler_params=_params(("parallel", "arbitrary")), name="kv_up_k",
    )(ckv_b, wkv2, kr)
    v = pl.pallas_call(
        _kv_up_v_kernel,
        grid=(rows // tm, nj),
        in_specs=[pl.BlockSpec((tm, kd), lambda i, j: (i, 0)),
                  pl.BlockSpec((None, kd, tn), lambda i, j: (l, 0, j + nj))],
        out_specs=pl.BlockSpec((KV_HEADS_PER_TILE, tm, 2 * MLA_V_DIM), lambda i, j: (j, i, 0)),
        out_shape=jax.ShapeDtypeStruct((MLA_HEADS, rows, 2 * MLA_V_DIM), BF16),
        compiler_params=_params(("parallel", "arbitrary")), name="kv_up_v",
    )(ckv_b, wkv2)
    return k, v


def _attn_prompt_kernel(q_ref, k_ref, v_ref, _, o_ref, m_ref, acc_ref, *, t):
    qi = pl.program_id(1)
    ki = pl.program_id(2)

    @pl.when(ki == 0)
    def _():
        m_ref[...] = jnp.full_like(m_ref, -jnp.inf)
        acc_ref[...] = jnp.zeros_like(acc_ref)

    def all_heads(bias):
        def head(h, carry):
            s = _dot_nt(q_ref[h], k_ref[h])
            if bias is not None:
                s = s + bias
            m_prev = m_ref[h]
            m_new = jnp.maximum(m_prev, jnp.max(s, axis=-1, keepdims=True))
            alpha = jnp.exp2(m_prev - m_new)
            p = jnp.exp2(s - jnp.tile(m_new, (1, t // LANES)))
            acc_ref[h] = jnp.tile(alpha, (1, 2)) * acc_ref[h] + _dot(p.astype(BF16), v_ref[h])
            m_ref[h] = m_new
            return carry
        lax.fori_loop(0, MLA_HEADS, head, 0, unroll=4)

    @pl.when(ki < qi)
    def _():
        all_heads(None)

    @pl.when(ki == qi)
    def _():
        rc = lax.broadcasted_iota(jnp.int32, (t, t), 0) // CHUNK
        cc = lax.broadcasted_iota(jnp.int32, (t, t), 1) // CHUNK
        all_heads(jnp.where(cc <= rc, 0.0, -jnp.inf))

    @pl.when(ki == pl.num_programs(2) - 1)
    def _():
        for h in range(MLA_HEADS):
            o = acc_ref[h, :, 0:MLA_V_DIM] / acc_ref[h, :, MLA_V_DIM:2 * MLA_V_DIM]
            o_ref[:, h * MLA_V_DIM:(h + 1) * MLA_V_DIM] = o.astype(o_ref.dtype)


def attention_prompt(q, k, v, prev, *, n_batch, seq_len, t=512):
    nt = seq_len // t
    kmap = lambda b, qi, ki: (0, b * nt + jnp.minimum(ki, qi), 0)
    return pl.pallas_call(
        functools.partial(_attn_prompt_kernel, t=t),
        grid=(n_batch, nt, nt),
        in_specs=[pl.BlockSpec((MLA_HEADS, t, Q_HEAD_PAD), lambda b, qi, ki: (0, b * nt + qi, 0)),
                  pl.BlockSpec((MLA_HEADS, t, Q_HEAD_PAD), kmap),
                  pl.BlockSpec((MLA_HEADS, t, 2 * MLA_V_DIM), kmap),
                  pl.BlockSpec(memory_space=pl.ANY)],
        out_specs=pl.BlockSpec((t, D_MLA), lambda b, qi, ki: (b * nt + qi, 0)),
        out_shape=jax.ShapeDtypeStruct(prev.shape, prev.dtype),
        input_output_aliases={3: 0},
        scratch_shapes=[pltpu.VMEM((MLA_HEADS, t, LANES), F32),
                        pltpu.VMEM((MLA_HEADS, t, 2 * MLA_V_DIM), F32)],
        compiler_params=_params(("parallel", "parallel", "arbitrary")), name="attn_prompt",
    )(q, k, v, prev)


def _attn_sample_kernel(q_ref, cp_ref, kp_ref, cn_ref, kn_ref, wk_ref, wv_ref, _, o_ref,
                        qa_ref, qr_ref, ol_ref, *, ls):
    for h in range(MLA_HEADS):
        rows = slice(h * ls, (h + 1) * ls)
        wk_h = wk_ref[:, h * MLA_NOPE_DIM:(h + 1) * MLA_NOPE_DIM]
        qa_ref[rows, :] = _dot_nt(q_ref[h, :, 0:MLA_NOPE_DIM], wk_h).astype(BF16)
        qr_ref[rows, :] = q_ref[h, :, MLA_NOPE_DIM:Q_HEAD_PAD]
    cp = cp_ref[...].astype(BF16)
    kp = kp_ref[...].astype(BF16)
    cn = cn_ref[...]
    kn = kn_ref[...].astype(BF16)
    qa = qa_ref[...]
    qr = qr_ref[...]
    s_p = _dot_nt(qa, cp) + _dot_nt(qr[:, :MLA_ROPE_DIM], kp)
    s_n = _dot_nt(qa, cn) + _dot_nt(qr, kn)
    m = jnp.maximum(jnp.max(s_p, axis=-1, keepdims=True), jnp.max(s_n, axis=-1, keepdims=True))
    p_p = jnp.exp2(s_p - m)
    p_n = jnp.exp2(s_n - m)
    den = jnp.sum(p_p, axis=-1, keepdims=True) + jnp.sum(p_n, axis=-1, keepdims=True)
    ol = (_dot(p_p.astype(BF16), cp) + _dot(p_n.astype(BF16), cn)) / den
    ol_ref[...] = ol.astype(BF16)
    for h in range(MLA_HEADS):
        wv_h = wv_ref[:, h * MLA_V_DIM:(h + 1) * MLA_V_DIM]
        o_ref[:, h * MLA_V_DIM:(h + 1) * MLA_V_DIM] = _dot(ol_ref[h * ls:(h + 1) * ls, :], wv_h).astype(o_ref.dtype)


def attention_sample(q, cache_ckv, cache_kr, ckv_b, kr, wkv2, l, prev, *, row0, n_batch, ls):
    past = cache_ckv.shape[2]
    rb0 = row0 // ls
    rmap = lambda b: (rb0 + b, 0)
    hk = MLA_HEADS * MLA_NOPE_DIM
    return pl.pallas_call(
        functools.partial(_attn_sample_kernel, ls=ls),
        grid=(n_batch,),
        in_specs=[pl.BlockSpec((MLA_HEADS, ls, Q_HEAD_PAD), lambda b: (0, rb0 + b, 0)),
                  pl.BlockSpec((None, None, past, MLA_KV_RANK), lambda b: (l, b, 0, 0)),
                  pl.BlockSpec((None, None, past, MLA_ROPE_DIM), lambda b: (l, b, 0, 0)),
                  pl.BlockSpec((ls, MLA_KV_RANK), rmap),
                  pl.BlockSpec((ls, LANES), rmap),
                  pl.BlockSpec((None, MLA_KV_RANK, hk), lambda b: (l, 0, 0)),
                  pl.BlockSpec((None, MLA_KV_RANK, hk), lambda b: (l, 0, 1)),
                  pl.BlockSpec(memory_space=pl.ANY)],
        out_specs=pl.BlockSpec((ls, D_MLA), rmap),
        out_shape=jax.ShapeDtypeStruct(prev.shape, prev.dtype),
        scratch_shapes=[pltpu.VMEM((MLA_HEADS * ls, MLA_KV_RANK), BF16),
                        pltpu.VMEM((MLA_HEADS * ls, LANES), BF16),
                        pltpu.VMEM((MLA_HEADS * ls, MLA_KV_RANK), BF16)],
        input_output_aliases={7: 0},
        compiler_params=_params(("parallel",)), name="attn_sample",
    )(q, cache_ckv, cache_kr, ckv_b, kr, wkv2, wkv2, prev)


def _rope_tables(pos):
    half = MLA_ROPE_DIM // 2
    inv = ROPE_THETA ** (-jnp.arange(half, dtype=F32) / half)
    ang = pos.astype(F32)[:, None] * inv[None, :]
    cos, sin = jnp.cos(ang), jnp.sin(ang)
    z = jnp.zeros((pos.shape[0], LANES - MLA_ROPE_DIM), F32)
    return jnp.concatenate([cos, cos, z], axis=1), jnp.concatenate([-sin, sin, z], axis=1)


def _swap_halves(w):
    half = w.shape[-1] // 2
    return jnp.concatenate([w[..., half:], w[..., :half]], axis=-1)


def kernel(x_prompt, x_sample, cache_mla_ckv, cache_mla_krope, state_pool, g_pre_mix, g_post_mix, g_pre_ffn, g_post_ffn, w_in, cm_ln_g, cm_ln_b, cm_ws, cm_bs, pool_w, pool_scale, mla_q_norm, mla_w_uq, mla_kv_norm, mla_w_ukv, w_out, ffn_w_gate, ffn_w_up, ffn_w_down):
    bp, lp, d = x_prompt.shape
    bs_, ls, _ = x_sample.shape
    past = cache_mla_ckv.shape[2]
    mp, ms = bp * lp, bs_ * ls
    m = mp + ms

    x = (x_prompt.reshape(mp, d), x_sample.reshape(ms, d))
    pos = jnp.concatenate([jnp.tile(jnp.arange(lp), bp), jnp.tile(past + jnp.arange(ls), bs_)])
    cos_t, sin_t = _rope_tables(pos)

    w_in_t = jnp.swapaxes(w_in, 1, 2)
    wq = mla_w_uq.reshape(DEPTH, MLA_Q_RANK, MLA_HEADS, MLA_NOPE_DIM + MLA_ROPE_DIM)
    wq_rope = wq[..., MLA_NOPE_DIM:]
    wq2 = jnp.concatenate([wq[..., :MLA_NOPE_DIM], wq_rope, _swap_halves(wq_rope)], axis=-1)
    wq2 = wq2.reshape(DEPTH, MLA_Q_RANK, MLA_HEADS * Q_HEAD_PAD).astype(BF16)
    wkv = mla_w_ukv.reshape(DEPTH, MLA_KV_RANK, MLA_HEADS, MLA_NOPE_DIM + MLA_V_DIM)
    wkv2 = jnp.concatenate([wkv[..., :MLA_NOPE_DIM].reshape(DEPTH, MLA_KV_RANK, -1),
                            wkv[..., MLA_NOPE_DIM:].reshape(DEPTH, MLA_KV_RANK, -1)], axis=-1).astype(BF16)
    wd_b = ffn_w_down.astype(BF16)
    pool_w_b = pool_w.astype(BF16)
    ws_s = cm_ws.at[:, :, CHUNK:, CHUNK:].set(cm_ws[:, :, :CHUNK, :CHUNK])
    ws2 = jnp.stack([cm_ws, ws_s], axis=1)
    bs_s = jnp.concatenate([cm_bs[:, :, :CHUNK], cm_bs[:, :, :CHUNK]], axis=-1)
    bs2 = jnp.swapaxes(jnp.stack([cm_bs, bs_s], axis=1), -1, -2)
    hist_p = jnp.zeros((1, bp, HIST_ROWS, D_POOL), F32)
    hist_s = jnp.pad(state_pool, ((0, 0), (0, 0), (HIST_ROWS - POOL_HIST, 0), (0, 0)))

    outs = {k: [] for k in ("pool_p", "pool_s")}
    ckv_all = (jnp.zeros((DEPTH * mp, MLA_KV_RANK), F32), jnp.zeros((DEPTH * ms, MLA_KV_RANK), F32))
    kr_all = (jnp.zeros((DEPTH * mp, MLA_ROPE_DIM), F32), jnp.zeros((DEPTH * ms, MLA_ROPE_DIM), F32))
    cmv_all = jnp.zeros((DEPTH * ms, D_CM), F32)
    b = jnp.zeros((m, D_POOL), BF16)
    c = jnp.zeros((m, D_MLA), BF16)
    h = prenorm(x[0], x[1], g_pre_mix[0])
    for l in range(DEPTH):
        last = l + 1 == DEPTH
        rest = matmul(h, w_in_t, l, n=OFF_KR, gelu_cols=OFF_POOL, wt=True, name="in_proj")
        kr, kr_all = kr_proj(h, w_in_t, l, cos_t, sin_t, kr_all, rows_first=mp)
        a, cmv_all = chunk_mlp(rest, cm_ln_g[l], cm_ln_b[l], ws2, bs2, l, mp, cmv_all)
        b = pool_mix(rest, hist_p, 0, pool_w_b, pool_scale[l], l, b, row0=0, n_seq=bp, seq_len=lp,
                     t_rows=512, pos0=0)
        b = pool_mix(rest, hist_s, l, pool_w_b, pool_scale[l], l, b, row0=mp, n_seq=bs_, seq_len=ls,
                     t_rows=ls, pos0=past)
        qn, ckv_b, ckv_all = latnorm(rest, mla_q_norm[l], mla_kv_norm[l], l, ckv_all, rows_first=mp)
        q = q_up(qn, wq2, l, cos_t, sin_t, scale=Q_SCALE)
        k_p, v_p = kv_up(ckv_b, kr, wkv2, l, mp)
        c = attention_prompt(q, k_p, v_p, c, n_batch=bp, seq_len=lp)
        c = attention_sample(q, cache_mla_ckv, cache_mla_krope, ckv_b, kr, wkv2, l, c,
                             row0=mp, n_batch=bs_, ls=ls)
        mix = mix_out(a, b, c, w_out, l)
        x, h2 = resnorm(x, mix, g_post_mix[l], g_pre_ffn[l])
        act = ffn_up(h2, ffn_w_gate, ffn_w_up, l)
        f = matmul(act, wd_b, l, out_dtype=BF16, tm=512, name="ffn_down")
        x, h = resnorm(x, f, g_post_ffn[l], None if last else g_pre_mix[l + 1],
                       split_rows=mp, split_out=last)

        tail = rest.reshape(m // ls, ls, rest.shape[1])[:, ls - POOL_HIST:, OFF_POOL:OFF_POOL + D_POOL]
        outs["pool_p"].append(tail[lp // ls - 1:mp // ls:lp // ls])
        outs["pool_s"].append(tail[mp // ls:])

    return (x[0].reshape(bp, lp, d), x[1].reshape(bs_, ls, d),
            ckv_all[0].reshape(DEPTH, bp, lp, MLA_KV_RANK), kr_all[0].reshape(DEPTH, bp, lp, MLA_ROPE_DIM),
            jnp.stack(outs["pool_p"]),
            ckv_all[1].reshape(DEPTH, bs_, ls, MLA_KV_RANK), kr_all[1].reshape(DEPTH, bs_, ls, MLA_ROPE_DIM),
            jnp.stack(outs["pool_s"]), cmv_all.reshape(DEPTH, bs_, ls, D_CM))
```

```python
import functools

import jax
import jax.numpy as jnp
from jax import lax
from jax.experimental import pallas as pl
from jax.experimental.pallas import tpu as pltpu

D_MODEL = 4096
DEPTH = 4
CHUNK = 64
EPS = 1e-6
CM_CHUNK = 128
CM_HEADS = 8
CM_HEAD_DIM = 128
D_CM = CM_HEADS * CM_HEAD_DIM
POOL_WINDOWS = (2, 4, 8, 16)
POOL_GROUP_DIM = 256
D_POOL = len(POOL_WINDOWS) * POOL_GROUP_DIM
POOL_HIST = max(POOL_WINDOWS) - 1
HIST_ROWS = POOL_HIST + 1
MLA_HEADS = 16
MLA_Q_RANK = 1024
MLA_KV_RANK = 512
MLA_NOPE_DIM = 128
MLA_ROPE_DIM = 64
MLA_V_DIM = 128
D_MLA = MLA_HEADS * MLA_V_DIM
MLA_SCALE = (MLA_NOPE_DIM + MLA_ROPE_DIM) ** -0.5
Q_SCALE = MLA_SCALE * 1.4426950408889634
ROPE_THETA = 10000.0
Q_HEAD_PAD = 256
OFF_V = D_CM
OFF_POOL = 2 * D_CM
OFF_Q = OFF_POOL + D_POOL
OFF_CKV = OFF_Q + MLA_Q_RANK
OFF_KR = OFF_CKV + MLA_KV_RANK
FF_TILE = 256

LANES = 128
VMEM_LIMIT = 56 * 1024 * 1024

BF16 = jnp.bfloat16
F32 = jnp.float32


def _params(sem):
    return pltpu.CompilerParams(dimension_semantics=sem, vmem_limit_bytes=VMEM_LIMIT)


def _rms(x, g):
    r = lax.rsqrt(jnp.mean(x * x, axis=-1, keepdims=True) + EPS)
    return (x * r) * g


def _dot(a, b):
    return jnp.dot(a, b, preferred_element_type=F32)


def _dot_nt(a, b):
    return lax.dot_general(a, b, (((1,), (1,)), ((), ())), preferred_element_type=F32)


def _rope_lanes(b, c, s):
    return b * c + pltpu.roll(b, 64, axis=1) * s


def _two_source_specs(tm, d, n_first):
    return [pl.BlockSpec((tm, d), lambda i: (jnp.minimum(i, n_first - 1), 0)),
            pl.BlockSpec((tm, d), lambda i: (jnp.maximum(i - n_first, 0), 0))]


def _prenorm_kernel(xp_ref, xs_ref, g_ref, h_ref, *, n_first):
    x = jnp.where(pl.program_id(0) < n_first, xp_ref[...], xs_ref[...])
    h_ref[...] = _rms(x, g_ref[...]).astype(BF16)


def prenorm(xp, xs, g, tm=512):
    d = xp.shape[1]
    m = xp.shape[0] + xs.shape[0]
    n_first = xp.shape[0] // tm
    return pl.pallas_call(
        functools.partial(_prenorm_kernel, n_first=n_first),
        grid=(m // tm,),
        in_specs=_two_source_specs(tm, d, n_first) + [pl.BlockSpec((1, d), lambda i: (0, 0))],
        out_specs=pl.BlockSpec((tm, d), lambda i: (i, 0)),
        out_shape=jax.ShapeDtypeStruct((m, d), BF16),
        compiler_params=_params(("arbitrary",)),
        name="prenorm",
    )(xp, xs, g.reshape(1, d))


def _resnorm_kernel(*refs, n_first, split_in, split_out, with_h):
    refs = list(refs)
    if split_in:
        xp_ref, xs_ref = refs[:2]
        x = jnp.where(pl.program_id(0) < n_first, xp_ref[...], xs_ref[...])
        refs = refs[2:]
    else:
        x = refs[0][...]
        refs = refs[1:]
    f_ref, gpost_ref = refs[:2]
    refs = refs[2:]
    if with_h:
        gpre_ref = refs[0]
        refs = refs[1:]
    xn = x + _rms(f_ref[...].astype(F32), gpost_ref[...])
    if split_out:
        _store_split(xn, refs[0], refs[1], n_first)
        refs = refs[2:]
    else:
        refs[0][...] = xn
        refs = refs[1:]
    if with_h:
        refs[0][...] = _rms(xn, gpre_ref[...]).astype(BF16)


def resnorm(x, f, g_post, g_pre, *, split_rows=None, split_out=False):
    split_in = isinstance(x, (tuple, list))
    tm = 256 if split_in or split_out else 384
    m, d = f.shape
    with_h = g_pre is not None
    n_first = (x[0].shape[0] if split_in else split_rows if split_out else m) // tm
    row = pl.BlockSpec((tm, d), lambda i: (i, 0))
    vec = pl.BlockSpec((1, d), lambda i: (0, 0))
    in_specs = (_two_source_specs(tm, d, n_first) if split_in else [row]) + [row, vec]
    args = (list(x) if split_in else [x]) + [f, g_post.reshape(1, d)]
    if with_h:
        in_specs.append(vec)
        args.append(g_pre.reshape(1, d))
    if split_out:
        out_specs = _two_source_specs(tm, d, n_first)
        out_shape = [jax.ShapeDtypeStruct((split_rows, d), F32), jax.ShapeDtypeStruct((m - split_rows, d), F32)]
    else:
        out_specs = [row]
        out_shape = [jax.ShapeDtypeStruct((m, d), F32)]
    if with_h:
        out_specs.append(row)
        out_shape.append(jax.ShapeDtypeStruct((m, d), BF16))
    outs = pl.pallas_call(
        functools.partial(_resnorm_kernel, n_first=n_first, split_in=split_in, split_out=split_out, with_h=with_h),
        grid=(m // tm,), in_specs=in_specs, out_specs=out_specs, out_shape=out_shape,
        compiler_params=_params(("arbitrary",)), name="resnorm",
    )(*args)
    xo = tuple(outs[:2]) if split_out else outs[0]
    return xo, (outs[-1] if with_h else None)


def _latnorm_kernel(q_ref, c_ref, gq_ref, gc_ref, _p, _s, cp_ref, cs_ref, qn_ref, cb_ref, *, n_first):
    qn_ref[...] = _rms(q_ref[...], gq_ref[...]).astype(BF16)
    cn = _rms(c_ref[...], gc_ref[...])
    cb_ref[...] = cn.astype(BF16)
    _store_split(cn, cp_ref, cs_ref, n_first)


def _store_split(val, first_ref, second_ref, n_first):
    @pl.when(pl.program_id(0) < n_first)
    def _():
        first_ref[...] = val

    @pl.when(pl.program_id(0) >= n_first)
    def _():
        second_ref[...] = val


def _layer_split_specs(tm, width, l, n_first, n_second):
    return [pl.BlockSpec((tm, width), lambda i, *_: (l * n_first + jnp.minimum(i, n_first - 1), 0)),
            pl.BlockSpec((tm, width), lambda i, *_: (l * n_second + jnp.maximum(i - n_first, 0), 0))]


def _layer_split_io(prev, n_in):
    shapes = [jax.ShapeDtypeStruct(p.shape, p.dtype) for p in prev]
    return shapes, [pl.BlockSpec(memory_space=pl.ANY)] * 2, {n_in: 0, n_in + 1: 1}


def latnorm(rest, gq, gc, l, prev, *, rows_first, tm=512):
    m = rest.shape[0]
    qb = OFF_Q // MLA_Q_RANK
    cb = OFF_CKV // MLA_KV_RANK
    n_first, n_second = rows_first // tm, (m - rows_first) // tm
    shapes, extra_specs, aliases = _layer_split_io(prev, 4)
    outs = pl.pallas_call(
        functools.partial(_latnorm_kernel, n_first=n_first),
        grid=(m // tm,),
        in_specs=[pl.BlockSpec((tm, MLA_Q_RANK), lambda i: (i, qb)),
                  pl.BlockSpec((tm, MLA_KV_RANK), lambda i: (i, cb)),
                  pl.BlockSpec((1, MLA_Q_RANK), lambda i: (0, 0)),
                  pl.BlockSpec((1, MLA_KV_RANK), lambda i: (0, 0))] + extra_specs,
        out_specs=_layer_split_specs(tm, MLA_KV_RANK, l, n_first, n_second)
        + [pl.BlockSpec((tm, MLA_Q_RANK), lambda i: (i, 0)),
           pl.BlockSpec((tm, MLA_KV_RANK), lambda i: (i, 0))],
        out_shape=shapes + [jax.ShapeDtypeStruct((m, MLA_Q_RANK), BF16),
                            jax.ShapeDtypeStruct((m, MLA_KV_RANK), BF16)],
        input_output_aliases=aliases,
        compiler_params=_params(("arbitrary",)), name="latnorm",
    )(rest, rest, gq.reshape(1, -1), gc.reshape(1, -1), *prev)
    return outs[2], outs[3], (outs[0], outs[1])


def _mm_kernel(a_ref, w_ref, o_ref, *, gelu_tiles, wt):
    r = (_dot_nt if wt else _dot)(a_ref[...], w_ref[...].astype(BF16))
    if gelu_tiles:
        @pl.when(pl.program_id(1) < gelu_tiles)
        def _():
            o_ref[...] = jax.nn.gelu(r).astype(o_ref.dtype)

        @pl.when(pl.program_id(1) >= gelu_tiles)
        def _():
            o_ref[...] = r.astype(o_ref.dtype)
    else:
        o_ref[...] = r.astype(o_ref.dtype)


def matmul(a, w, l, *, n=None, gelu_cols=0, wt=False, out_dtype=F32, tm=1024, tn=512, name="mm"):
    m, k = a.shape
    n = w.shape[1 if wt else 2] if n is None else n
    wspec = (pl.BlockSpec((None, tn, k), lambda i, j: (l, j, 0)) if wt
             else pl.BlockSpec((None, k, tn), lambda i, j: (l, 0, j)))
    return pl.pallas_call(
        functools.partial(_mm_kernel, gelu_tiles=gelu_cols // tn, wt=wt),
        grid=(m // tm, n // tn),
        in_specs=[pl.BlockSpec((tm, k), lambda i, j: (i, 0)), wspec],
        out_specs=pl.BlockSpec((tm, tn), lambda i, j: (i, j)),
        out_shape=jax.ShapeDtypeStruct((m, n), out_dtype),
        compiler_params=_params(("parallel", "arbitrary")), name=name,
    )(a, w)


def _ffn_up_kernel(h_ref, wg_ref, wu_ref, o_ref):
    h = h_ref[...]
    g = _dot(h, wg_ref[...].astype(BF16))
    u = _dot(h, wu_ref[...].astype(BF16))
    o_ref[...] = (jax.nn.silu(g) * u).astype(o_ref.dtype)


def ffn_up(h, wg, wu, l, tm=1536, tn=FF_TILE):
    m, k = h.shape
    n = wg.shape[2]
    wspec = pl.BlockSpec((None, k, tn), lambda i, j: (l, 0, j))
    return pl.pallas_call(
        _ffn_up_kernel,
        grid=(m // tm, n // tn),
        in_specs=[pl.BlockSpec((tm, k), lambda i, j: (i, 0)), wspec, wspec],
        out_specs=pl.BlockSpec((tm, tn), lambda i, j: (i, j)),
        out_shape=jax.ShapeDtypeStruct((m, n), BF16),
        compiler_params=_params(("parallel", "arbitrary")), name="ffn_up",
    )(h, wg, wu)


def _q_up_kernel(a_ref, w_ref, c_ref, s_ref, o_ref, *, scale):
    r = _dot(a_ref[...], w_ref[...]) * scale
    c = c_ref[...]
    s = s_ref[...]
    for h in range(r.shape[1] // Q_HEAD_PAD):
        lo = h * Q_HEAD_PAD
        o_ref[h, :, 0:MLA_NOPE_DIM] = r[:, lo:lo + MLA_NOPE_DIM].astype(o_ref.dtype)
        b = r[:, lo + MLA_NOPE_DIM:lo + Q_HEAD_PAD]
        o_ref[h, :, MLA_NOPE_DIM:Q_HEAD_PAD] = _rope_lanes(b, c, s).astype(o_ref.dtype)


Q_HEADS_PER_TILE = 4


def q_up(a, w, l, cos_t, sin_t, *, scale, tm=1024):
    m, k = a.shape
    n = w.shape[2]
    tn = Q_HEADS_PER_TILE * Q_HEAD_PAD
    return pl.pallas_call(
        functools.partial(_q_up_kernel, scale=scale),
        grid=(m // tm, n // tn),
        in_specs=[pl.BlockSpec((tm, k), lambda i, j: (i, 0)),
                  pl.BlockSpec((None, k, tn), lambda i, j: (l, 0, j)),
                  pl.BlockSpec((tm, LANES), lambda i, j: (i, 0)),
                  pl.BlockSpec((tm, LANES), lambda i, j: (i, 0))],
        out_specs=pl.BlockSpec((Q_HEADS_PER_TILE, tm, Q_HEAD_PAD), lambda i, j: (j, i, 0)),
        out_shape=jax.ShapeDtypeStruct((n // Q_HEAD_PAD, m, Q_HEAD_PAD), BF16),
        compiler_params=_params(("parallel", "arbitrary")), name="q_up",
    )(a, w, cos_t, sin_t)


def _kr_kernel(a_ref, w_ref, c_ref, s_ref, _p, _s, kp_ref, ks_ref, o_ref, *, n_first):
    r = _dot_nt(a_ref[...], w_ref[...].astype(BF16))
    lane = lax.broadcasted_iota(jnp.int32, r.shape, 1)
    half = MLA_ROPE_DIM // 2
    partner = jnp.where(lane < half, pltpu.roll(r, LANES - half, axis=1), pltpu.roll(r, half, axis=1))
    kr = jnp.where(lane < MLA_ROPE_DIM, r * c_ref[...] + partner * s_ref[...], 0.0)
    o_ref[...] = kr
    _store_split(kr[:, :MLA_ROPE_DIM], kp_ref, ks_ref, n_first)


def kr_proj(h, w_in_t, l, cos_t, sin_t, prev, *, rows_first, tm=1024):
    m, k = h.shape
    n_first, n_second = rows_first // tm, (m - rows_first) // tm
    shapes, extra_specs, aliases = _layer_split_io(prev, 4)
    outs = pl.pallas_call(
        functools.partial(_kr_kernel, n_first=n_first),
        grid=(m // tm,),
        in_specs=[pl.BlockSpec((tm, k), lambda i: (i, 0)),
                  pl.BlockSpec((None, LANES, k), lambda i: (l, OFF_KR // LANES, 0)),
                  pl.BlockSpec((tm, LANES), lambda i: (i, 0)),
                  pl.BlockSpec((tm, LANES), lambda i: (i, 0))] + extra_specs,
        out_specs=_layer_split_specs(tm, MLA_ROPE_DIM, l, n_first, n_second)
        + [pl.BlockSpec((tm, LANES), lambda i: (i, 0))],
        out_shape=shapes + [jax.ShapeDtypeStruct((m, LANES), F32)],
        input_output_aliases=aliases,
        compiler_params=_params(("arbitrary",)), name="in_kr",
    )(h, w_in_t, cos_t, sin_t, *prev)
    return outs[2], (outs[0], outs[1])


def _mix_out_kernel(a_ref, b_ref, c_ref, wa_ref, wb_ref, wc_ref, o_ref):
    o_ref[...] = (_dot(a_ref[...], wa_ref[...].astype(BF16)) + _dot(b_ref[...], wb_ref[...].astype(BF16))
                  + _dot(c_ref[...], wc_ref[...].astype(BF16))).astype(o_ref.dtype)


def mix_out(a, b, c, w_out, l, tm=1024, tn=512):
    m = a.shape[0]
    n = w_out.shape[2]
    ka, kb, kc = a.shape[1], b.shape[1], c.shape[1]
    return pl.pallas_call(
        _mix_out_kernel,
        grid=(m // tm, n // tn),
        in_specs=[pl.BlockSpec((tm, ka), lambda i, j: (i, 0)),
                  pl.BlockSpec((tm, kb), lambda i, j: (i, 0)),
                  pl.BlockSpec((tm, kc), lambda i, j: (i, 0)),
                  pl.BlockSpec((None, ka, tn), lambda i, j: (l, 0, j)),
                  pl.BlockSpec((None, kb, tn), lambda i, j: (l, ka // kb, j)),
                  pl.BlockSpec((None, kc, tn), lambda i, j: (l, (ka + kb) // kc, j))],
        out_specs=pl.BlockSpec((tm, tn), lambda i, j: (i, j)),
        out_shape=jax.ShapeDtypeStruct((m, n), BF16),
        compiler_params=_params(("parallel", "arbitrary")), name="mix_out",
    )(a, b, c, w_out, w_out, w_out)


def _chunk_mlp_kernel(u_ref, v_ref, g_ref, b_ref, w_ref, bs_ref, _, a_ref, vn_ref, *, first_sample_block):
    is_sample = pl.program_id(0) >= first_sample_block
    ri = lax.broadcasted_iota(jnp.int32, (CM_CHUNK, CM_CHUNK), 0) // CHUNK
    ci = lax.broadcasted_iota(jnp.int32, (CM_CHUNK, CM_CHUNK), 1) // CHUNK
    max_lag = jnp.where(is_sample, 0, 1)
    keep = (ri - ci >= 0) & (ri - ci <= max_lag)
    bs = bs_ref[0]
    ws = [jnp.where(keep, w_ref[0, h], 0.0).astype(BF16) for h in range(CM_HEADS)]
    for c in range(u_ref.shape[0] // CM_CHUNK):
        rows = slice(c * CM_CHUNK, (c + 1) * CM_CHUNK)
        v = v_ref[rows, :]
        mu = jnp.mean(v, axis=-1, keepdims=True)
        d = v - mu
        var = jnp.mean(d * d, axis=-1, keepdims=True)
        vn = (d * lax.rsqrt(var + EPS)) * g_ref[...] + b_ref[...]
        @pl.when(is_sample)
        def _():
            vn_ref[rows, :] = vn

        vb = vn.astype(BF16)
        for h in range(CM_HEADS):
            cols = slice(h * CM_HEAD_DIM, (h + 1) * CM_HEAD_DIM)
            sv = _dot(ws[h], vb[:, cols]) + bs[:, h:h + 1]
            a_ref[rows, cols] = (u_ref[rows, cols] * sv).astype(a_ref.dtype)


CM_ROWS = 4 * CM_CHUNK


def chunk_mlp(uv, ln_g, ln_b, ws2, bs2, l, first_sample_row, prev):
    m = uv.shape[0]
    first_sample_block = first_sample_row // CM_ROWS
    n_sample = m // CM_ROWS - first_sample_block
    sel = lambda c: jnp.where(c >= first_sample_block, 1, 0)
    return pl.pallas_call(
        functools.partial(_chunk_mlp_kernel, first_sample_block=first_sample_block),
        grid=(m // CM_ROWS,),
        in_specs=[pl.BlockSpec((CM_ROWS, D_CM), lambda c: (c, 0)),
                  pl.BlockSpec((CM_ROWS, D_CM), lambda c: (c, 1)),
                  pl.BlockSpec((1, D_CM), lambda c: (0, 0)),
                  pl.BlockSpec((1, D_CM), lambda c: (0, 0)),
                  pl.BlockSpec((None, 1, CM_HEADS, CM_CHUNK, CM_CHUNK), lambda c: (l, sel(c), 0, 0, 0)),
                  pl.BlockSpec((None, 1, CM_CHUNK, CM_HEADS), lambda c: (l, sel(c), 0, 0)),
                  pl.BlockSpec(memory_space=pl.ANY)],
        out_specs=[pl.BlockSpec((CM_ROWS, D_CM), lambda c: (c, 0)),
                   pl.BlockSpec((CM_ROWS, D_CM),
                                lambda c: (l * n_sample + jnp.maximum(c - first_sample_block, 0), 0))],
        out_shape=[jax.ShapeDtypeStruct((m, D_CM), BF16), jax.ShapeDtypeStruct(prev.shape, prev.dtype)],
        input_output_aliases={6: 1},
        compiler_params=_params(("arbitrary",)), name="chunk_mlp",
    )(uv, uv, ln_g.reshape(1, -1), ln_b.reshape(1, -1), ws2, bs2, prev)


def _pool_kernel(x_ref, hist_ref, w_ref, sc_ref, _, o_ref, e_ref, *, t_rows, pos0):
    t = pl.program_id(1)

    @pl.when(t == 0)
    def _():
        e_ref[0:HIST_ROWS, :] = hist_ref[0]

    x = x_ref[...]
    e_ref[HIST_ROWS:HIST_ROWS + t_rows, :] = x
    pos = pos0 + t * t_rows + lax.broadcasted_iota(jnp.int32, (t_rows, 1), 0)
    for g, w in enumerate(POOL_WINDOWS):
        cols = slice(g * POOL_GROUP_DIM, (g + 1) * POOL_GROUP_DIM)
        acc = x[:, cols]
        for k in range(1, w):
            acc = acc + e_ref[HIST_ROWS - k:HIST_ROWS - k + t_rows, cols]
        cnt = jnp.minimum(w, pos + 1).astype(F32)
        mg = (acc / cnt - x[:, cols]).astype(BF16)
        y = _dot(mg, w_ref[g]) * sc_ref[:, cols]
        o_ref[:, cols] = y.astype(o_ref.dtype)
    e_ref[0:HIST_ROWS, :] = e_ref[t_rows:t_rows + HIST_ROWS, :]


def pool_mix(rest, hist, hl, pool_w, pool_scale, l, prev, *, row0, n_seq, seq_len, t_rows, pos0):
    n_t = seq_len // t_rows
    b0 = row0 // t_rows
    return pl.pallas_call(
        functools.partial(_pool_kernel, t_rows=t_rows, pos0=pos0),
        grid=(n_seq, n_t),
        in_specs=[pl.BlockSpec((t_rows, D_POOL), lambda s, t: (b0 + s * n_t + t, OFF_POOL // D_POOL)),
                  pl.BlockSpec((None, 1, HIST_ROWS, D_POOL), lambda s, t: (hl, s, 0, 0)),
                  pl.BlockSpec((None,) + pool_w.shape[1:], lambda s, t: (l, 0, 0, 0)),
                  pl.BlockSpec((1, D_POOL), lambda s, t: (0, 0)),
                  pl.BlockSpec(memory_space=pl.ANY)],
        out_specs=pl.BlockSpec((t_rows, D_POOL), lambda s, t: (b0 + s * n_t + t, 0)),
        out_shape=jax.ShapeDtypeStruct(prev.shape, prev.dtype),
        scratch_shapes=[pltpu.VMEM((HIST_ROWS + t_rows, D_POOL), F32)],
        input_output_aliases={4: 0},
        compiler_params=_params(("parallel", "arbitrary")), name="pool_mix",
    )(rest, hist, pool_w, pool_scale.reshape(1, -1), prev)


KV_HEADS_PER_TILE = 8


def _kv_up_k_kernel(c_ref, w_ref, kr_ref, o_ref):
    r = _dot(c_ref[...], w_ref[...])
    kr = kr_ref[...].astype(BF16)
    for hh in range(KV_HEADS_PER_TILE):
        o_ref[hh, :, 0:MLA_NOPE_DIM] = r[:, hh * MLA_NOPE_DIM:(hh + 1) * MLA_NOPE_DIM].astype(BF16)
        o_ref[hh, :, MLA_NOPE_DIM:Q_HEAD_PAD] = kr


def _kv_up_v_kernel(c_ref, w_ref, o_ref):
    r = _dot(c_ref[...], w_ref[...])
    ones = jnp.ones((r.shape[0], MLA_V_DIM), BF16)
    for hh in range(KV_HEADS_PER_TILE):
        o_ref[hh, :, 0:MLA_V_DIM] = r[:, hh * MLA_V_DIM:(hh + 1) * MLA_V_DIM].astype(BF16)
        o_ref[hh, :, MLA_V_DIM:2 * MLA_V_DIM] = ones


def kv_up(ckv_b, kr, wkv2, l, rows, tm=1024):
    kd = ckv_b.shape[1]
    tn = KV_HEADS_PER_TILE * MLA_NOPE_DIM
    nj = MLA_HEADS // KV_HEADS_PER_TILE
    k = pl.pallas_call(
        _kv_up_k_kernel,
        grid=(rows // tm, nj),
        in_specs=[pl.BlockSpec((tm, kd), lambda i, j: (i, 0)),
                  pl.BlockSpec((None, kd, tn), lambda i, j: (l, 0, j)),
                  pl.BlockSpec((tm, LANES), lambda i, j: (i, 0))],
        out_specs=pl.BlockSpec((KV_HEADS_PER_TILE, tm, Q_HEAD_PAD), lambda i, j: (j, i, 0)),
        out_shape=jax.ShapeDtypeStruct((MLA_HEADS, rows, Q_HEAD_PAD), BF16),
        compiler_params=_params(("parallel", "arbitrary")), name="kv_up_k",
    )(ckv_b, wkv2, kr)
    v = pl.pallas_call(
        _kv_up_v_kernel,
        grid=(rows // tm, nj),
        in_specs=[pl.BlockSpec((tm, kd), lambda i, j: (i, 0)),
                  pl.BlockSpec((None, kd, tn), lambda i, j: (l, 0, j + nj))],
        out_specs=pl.BlockSpec((KV_HEADS_PER_TILE, tm, 2 * MLA_V_DIM), lambda i, j: (j, i, 0)),
        out_shape=jax.ShapeDtypeStruct((MLA_HEADS, rows, 2 * MLA_V_DIM), BF16),
        compiler_params=_params(("parallel", "arbitrary")), name="kv_up_v",
    )(ckv_b, wkv2)
    return k, v


def _attn_prompt_kernel(q_ref, k_ref, v_ref, _, o_ref, m_ref, acc_ref, *, t):
    qi = pl.program_id(1)
    ki = pl.program_id(2)

    @pl.when(ki == 0)
    def _():
        m_ref[...] = jnp.full_like(m_ref, -jnp.inf)
        acc_ref[...] = jnp.zeros_like(acc_ref)

    def all_heads(bias):
        def head(h, carry):
            s = _dot_nt(q_ref[h], k_ref[h])
            if bias is not None:
                s = s + bias
            m_prev = m_ref[h]
            m_new = jnp.maximum(m_prev, jnp.max(s, axis=-1, keepdims=True))
            alpha = jnp.exp2(m_prev - m_new)
            p = jnp.exp2(s - jnp.tile(m_new, (1, t // LANES)))
            acc_ref[h] = jnp.tile(alpha, (1, 2)) * acc_ref[h] + _dot(p.astype(BF16), v_ref[h])
            m_ref[h] = m_new
            return carry
        lax.fori_loop(0, MLA_HEADS, head, 0, unroll=16)

    @pl.when(ki < qi)
    def _():
        all_heads(None)

    @pl.when(ki == qi)
    def _():
        rc = lax.broadcasted_iota(jnp.int32, (t, t), 0) // CHUNK
        cc = lax.broadcasted_iota(jnp.int32, (t, t), 1) // CHUNK
        all_heads(jnp.where(cc <= rc, 0.0, -jnp.inf))

    @pl.when(ki == pl.num_programs(2) - 1)
    def _():
        for h in range(MLA_HEADS):
            o = acc_ref[h, :, 0:MLA_V_DIM] / acc_ref[h, :, MLA_V_DIM:2 * MLA_V_DIM]
            o_ref[:, h * MLA_V_DIM:(h + 1) * MLA_V_DIM] = o.astype(o_ref.dtype)


def attention_prompt(q, k, v, prev, *, n_batch, seq_len, t=512):
    nt = seq_len // t
    kmap = lambda b, qi, ki: (0, b * nt + jnp.minimum(ki, qi), 0)
    return pl.pallas_call(
        functools.partial(_attn_prompt_kernel, t=t),
        grid=(n_batch, nt, nt),
        in_specs=[pl.BlockSpec((MLA_HEADS, t, Q_HEAD_PAD), lambda b, qi, ki: (0, b * nt + qi, 0)),
                  pl.BlockSpec((MLA_HEADS, t, Q_HEAD_PAD), kmap),
                  pl.BlockSpec((MLA_HEADS, t, 2 * MLA_V_DIM), kmap),
                  pl.BlockSpec(memory_space=pl.ANY)],
        out_specs=pl.BlockSpec((t, D_MLA), lambda b, qi, ki: (b * nt + qi, 0)),
        out_shape=jax.ShapeDtypeStruct(prev.shape, prev.dtype),
        input_output_aliases={3: 0},
        scratch_shapes=[pltpu.VMEM((MLA_HEADS, t, LANES), F32),
                        pltpu.VMEM((MLA_HEADS, t, 2 * MLA_V_DIM), F32)],
        compiler_params=_params(("parallel", "parallel", "arbitrary")), name="attn_prompt",
    )(q, k, v, prev)


def _attn_sample_kernel(q_ref, cp_ref, kp_ref, cn_ref, kn_ref, wk_ref, wv_ref, _, o_ref,
                        qa_ref, qr_ref, ol_ref, *, ls):
    for h in range(MLA_HEADS):
        rows = slice(h * ls, (h + 1) * ls)
        wk_h = wk_ref[:, h * MLA_NOPE_DIM:(h + 1) * MLA_NOPE_DIM]
        qa_ref[rows, :] = _dot_nt(q_ref[h, :, 0:MLA_NOPE_DIM], wk_h).astype(BF16)
        qr_ref[rows, :] = q_ref[h, :, MLA_NOPE_DIM:Q_HEAD_PAD]
    cp = cp_ref[...].astype(BF16)
    kp = kp_ref[...].astype(BF16)
    cn = cn_ref[...]
    kn = kn_ref[...].astype(BF16)
    qa = qa_ref[...]
    qr = qr_ref[...]
    s_p = _dot_nt(qa, cp) + _dot_nt(qr[:, :MLA_ROPE_DIM], kp)
    s_n = _dot_nt(qa, cn) + _dot_nt(qr, kn)
    m = jnp.maximum(jnp.max(s_p, axis=-1, keepdims=True), jnp.max(s_n, axis=-1, keepdims=True))
    p_p = jnp.exp2(s_p - m)
    p_n = jnp.exp2(s_n - m)
    den = jnp.sum(p_p, axis=-1, keepdims=True) + jnp.sum(p_n, axis=-1, keepdims=True)
    ol = (_dot(p_p.astype(BF16), cp) + _dot(p_n.astype(BF16), cn)) / den
    ol_ref[...] = ol.astype(BF16)
    for h in range(MLA_HEADS):
        wv_h = wv_ref[:, h * MLA_V_DIM:(h + 1) * MLA_V_DIM]
        o_ref[:, h * MLA_V_DIM:(h + 1) * MLA_V_DIM] = _dot(ol_ref[h * ls:(h + 1) * ls, :], wv_h).astype(o_ref.dtype)


def attention_sample(q, cache_ckv, cache_kr, ckv_b, kr, wkv2, l, prev, *, row0, n_batch, ls):
    past = cache_ckv.shape[2]
    rb0 = row0 // ls
    rmap = lambda b: (rb0 + b, 0)
    hk = MLA_HEADS * MLA_NOPE_DIM
    return pl.pallas_call(
        functools.partial(_attn_sample_kernel, ls=ls),
        grid=(n_batch,),
        in_specs=[pl.BlockSpec((MLA_HEADS, ls, Q_HEAD_PAD), lambda b: (0, rb0 + b, 0)),
                  pl.BlockSpec((None, None, past, MLA_KV_RANK), lambda b: (l, b, 0, 0)),
                  pl.BlockSpec((None, None, past, MLA_ROPE_DIM), lambda b: (l, b, 0, 0)),
                  pl.BlockSpec((ls, MLA_KV_RANK), rmap),
                  pl.BlockSpec((ls, LANES), rmap),
                  pl.BlockSpec((None, MLA_KV_RANK, hk), lambda b: (l, 0, 0)),
                  pl.BlockSpec((None, MLA_KV_RANK, hk), lambda b: (l, 0, 1)),
                  pl.BlockSpec(memory_space=pl.ANY)],
        out_specs=pl.BlockSpec((ls, D_MLA), rmap),
        out_shape=jax.ShapeDtypeStruct(prev.shape, prev.dtype),
        scratch_shapes=[pltpu.VMEM((MLA_HEADS * ls, MLA_KV_RANK), BF16),
                        pltpu.VMEM((MLA_HEADS * ls, LANES), BF16),
                        pltpu.VMEM((MLA_HEADS * ls, MLA_KV_RANK), BF16)],
        input_output_aliases={7: 0},
        compiler_params=_params(("parallel",)), name="attn_sample",
    )(q, cache_ckv, cache_kr, ckv_b, kr, wkv2, wkv2, prev)


def _rope_tables(pos):
    half = MLA_ROPE_DIM // 2
    inv = ROPE_THETA ** (-jnp.arange(half, dtype=F32) / half)
    ang = pos.astype(F32)[:, None] * inv[None, :]
    cos, sin = jnp.cos(ang), jnp.sin(ang)
    z = jnp.zeros((pos.shape[0], LANES - MLA_ROPE_DIM), F32)
    return jnp.concatenate([cos, cos, z], axis=1), jnp.concatenate([-sin, sin, z], axis=1)


def _swap_halves(w):
    half = w.shape[-1] // 2
    return jnp.concatenate([w[..., half:], w[..., :half]], axis=-1)


def kernel(x_prompt, x_sample, cache_mla_ckv, cache_mla_krope, state_pool, g_pre_mix, g_post_mix, g_pre_ffn, g_post_ffn, w_in, cm_ln_g, cm_ln_b, cm_ws, cm_bs, pool_w, pool_scale, mla_q_norm, mla_w_uq, mla_kv_norm, mla_w_ukv, w_out, ffn_w_gate, ffn_w_up, ffn_w_down):
    bp, lp, d = x_prompt.shape
    bs_, ls, _ = x_sample.shape
    past = cache_mla_ckv.shape[2]
    mp, ms = bp * lp, bs_ * ls
    m = mp + ms

    x = (x_prompt.reshape(mp, d), x_sample.reshape(ms, d))
    pos = jnp.concatenate([jnp.tile(jnp.arange(lp), bp), jnp.tile(past + jnp.arange(ls), bs_)])
    cos_t, sin_t = _rope_tables(pos)

    w_in_t = jnp.swapaxes(w_in, 1, 2)
    wq = mla_w_uq.reshape(DEPTH, MLA_Q_RANK, MLA_HEADS, MLA_NOPE_DIM + MLA_ROPE_DIM)
    wq_rope = wq[..., MLA_NOPE_DIM:]
    wq2 = jnp.concatenate([wq[..., :MLA_NOPE_DIM], wq_rope, _swap_halves(wq_rope)], axis=-1)
    wq2 = wq2.reshape(DEPTH, MLA_Q_RANK, MLA_HEADS * Q_HEAD_PAD).astype(BF16)
    wkv = mla_w_ukv.reshape(DEPTH, MLA_KV_RANK, MLA_HEADS, MLA_NOPE_DIM + MLA_V_DIM)
    wkv2 = jnp.concatenate([wkv[..., :MLA_NOPE_DIM].reshape(DEPTH, MLA_KV_RANK, -1),
                            wkv[..., MLA_NOPE_DIM:].reshape(DEPTH, MLA_KV_RANK, -1)], axis=-1).astype(BF16)
    wd_b = ffn_w_down.astype(BF16)
    pool_w_b = pool_w.astype(BF16)
    ws_s = cm_ws.at[:, :, CHUNK:, CHUNK:].set(cm_ws[:, :, :CHUNK, :CHUNK])
    ws2 = jnp.stack([cm_ws, ws_s], axis=1)
    bs_s = jnp.concatenate([cm_bs[:, :, :CHUNK], cm_bs[:, :, :CHUNK]], axis=-1)
    bs2 = jnp.swapaxes(jnp.stack([cm_bs, bs_s], axis=1), -1, -2)
    hist_p = jnp.zeros((1, bp, HIST_ROWS, D_POOL), F32)
    hist_s = jnp.pad(state_pool, ((0, 0), (0, 0), (HIST_ROWS - POOL_HIST, 0), (0, 0)))

    outs = {k: [] for k in ("pool_p", "pool_s")}
    ckv_all = (jnp.zeros((DEPTH * mp, MLA_KV_RANK), F32), jnp.zeros((DEPTH * ms, MLA_KV_RANK), F32))
    kr_all = (jnp.zeros((DEPTH * mp, MLA_ROPE_DIM), F32), jnp.zeros((DEPTH * ms, MLA_ROPE_DIM), F32))
    cmv_all = jnp.zeros((DEPTH * ms, D_CM), F32)
    b = jnp.zeros((m, D_POOL), BF16)
    c = jnp.zeros((m, D_MLA), BF16)
    h = prenorm(x[0], x[1], g_pre_mix[0])
    for l in range(DEPTH):
        last = l + 1 == DEPTH
        rest = matmul(h, w_in_t, l, n=OFF_KR, gelu_cols=OFF_POOL, wt=True, name="in_proj")
        kr, kr_all = kr_proj(h, w_in_t, l, cos_t, sin_t, kr_all, rows_first=mp)
        a, cmv_all = chunk_mlp(rest, cm_ln_g[l], cm_ln_b[l], ws2, bs2, l, mp, cmv_all)
        b = pool_mix(rest, hist_p, 0, pool_w_b, pool_scale[l], l, b, row0=0, n_seq=bp, seq_len=lp,
                     t_rows=512, pos0=0)
        b = pool_mix(rest, hist_s, l, pool_w_b, pool_scale[l], l, b, row0=mp, n_seq=bs_, seq_len=ls,
                     t_rows=ls, pos0=past)
        qn, ckv_b, ckv_all = latnorm(rest, mla_q_norm[l], mla_kv_norm[l], l, ckv_all, rows_first=mp)
        q = q_up(qn, wq2, l, cos_t, sin_t, scale=Q_SCALE)
        k_p, v_p = kv_up(ckv_b, kr, wkv2, l, mp)
        c = attention_prompt(q, k_p, v_p, c, n_batch=bp, seq_len=lp)
        c = attention_sample(q, cache_mla_ckv, cache_mla_krope, ckv_b, kr, wkv2, l, c,
                             row0=mp, n_batch=bs_, ls=ls)
        mix = mix_out(a, b, c, w_out, l)
        x, h2 = resnorm(x, mix, g_post_mix[l], g_pre_ffn[l])
        act = ffn_up(h2, ffn_w_gate, ffn_w_up, l)
        f = matmul(act, wd_b, l, out_dtype=BF16, tm=512, name="ffn_down")
        x, h = resnorm(x, f, g_post_ffn[l], None if last else g_pre_mix[l + 1],
                       split_rows=mp, split_out=last)

        tail = rest.reshape(m // ls, ls, rest.shape[1])[:, ls - POOL_HIST:, OFF_POOL:OFF_POOL + D_POOL]
        outs["pool_p"].append(tail[lp // ls - 1:mp // ls:lp // ls])
        outs["pool_s"].append(tail[mp // ls:])

    return (x[0].reshape(bp, lp, d), x[1].reshape(bs_, ls, d),
            ckv_all[0].reshape(DEPTH, bp, lp, MLA_KV_RANK), kr_all[0].reshape(DEPTH, bp, lp, MLA_ROPE_DIM),
            jnp.stack(outs["pool_p"]),
            ckv_all[1].reshape(DEPTH, bs_, ls, MLA_KV_RANK), kr_all[1].reshape(DEPTH, bs_, ls, MLA_ROPE_DIM),
            jnp.stack(outs["pool_s"]), cmv_all.reshape(DEPTH, bs_, ls, D_CM))
```
